```python
import math
import numpy as np
import jax
import jax.numpy as jnp
from jax import lax

D_MODEL = 2048
BATCH = 2
SEQ = 16384
DEPTH = 2

GRID_W = 64
HY_W = 512
HY_ORDER = 2
HY_BANDS = 16
HY_EMB = 1 + 2 * HY_BANDS
HY_FFN = 64
HY_DECAY_TARGET = 1e-2
HY_FAST_DECAY = 0.3
HY_SLOW_DECAY = 1.5
NA_HEADS = 8
NA_HEAD_DIM = 64
NA_W = NA_HEADS * NA_HEAD_DIM
NA_WIN_R = 8
NA_WIN_C = 16
MLA_HEADS = 8
MLA_Q_RANK = 512
MLA_KV_RANK = 256
MLA_NOPE = 128
MLA_ROPE = 64
MLA_V = 128
MLA_W = MLA_HEADS * MLA_V
ROPE_BASE = 10000.0
Q_BLOCK = 128
MIX_W = HY_W + NA_W + MLA_W
IN_W = 3 * HY_W + 3 * NA_W + MLA_Q_RANK + MLA_KV_RANK + MLA_ROPE
N_EXPERTS = 64
N_GROUPS = 8
TOP_K = 2
D_EXPERT = 512
MOE_BLOCK = 128
ALPHA = (2 * DEPTH) ** 0.25
BETA = (8 * DEPTH) ** -0.25
EPS = 1e-5

kernel_name = 'hybrid_hyena_natten_mla_moe_encoder'

F32 = jnp.float32


def layer_norm(x, g, b):
    xf = x.astype(F32)
    mu = jnp.mean(xf, -1, keepdims=True)
    var = jnp.mean(jnp.square(xf - mu), -1, keepdims=True)
    return ((xf - mu) * lax.rsqrt(var + EPS) * g.astype(F32) + b.astype(F32)).astype(x.dtype)


def rms_norm(x, g):
    xf = x.astype(F32)
    return (xf * lax.rsqrt(jnp.mean(xf * xf, -1, keepdims=True) + EPS) * g.astype(F32)).astype(x.dtype)


def rope(x, cos, sin):
    x1, x2 = jnp.split(x, 2, axis=-1)
    return jnp.concatenate([x1 * cos - x2 * sin, x1 * sin + x2 * cos], -1).astype(x.dtype)


def hyena_filter_spectra(L, w1, b1, f1, w2, b2, f2, w3):
    t_idx = jnp.arange(L, dtype=F32)
    t = t_idx / L
    bands = jnp.linspace(1e-4, HY_BANDS - 1, HY_BANDS, dtype=F32)
    ang = (2.0 * math.pi) * t[:, None] * bands[None, :]
    feats = jnp.concatenate([t[:, None], jnp.cos(ang), -jnp.sin(ang)], -1)
    h = jnp.sin(f1.astype(F32) * (feats @ w1.astype(F32) + b1.astype(F32)))
    h = jnp.sin(f2.astype(F32) * (h @ w2.astype(F32) + b2.astype(F32)))
    h = (h @ w3.astype(F32)).reshape(L, 2, HY_ORDER, HY_W)
    deltas = jnp.abs(jnp.linspace(math.log(HY_DECAY_TARGET) / HY_SLOW_DECAY,
                                  math.log(HY_DECAY_TARGET) / HY_FAST_DECAY, HY_W, dtype=F32))
    h = h * jnp.exp(-t[:, None] * deltas[None, :])[:, None, None, :]
    fwd, bwd = h[:, 0], h[:, 1]
    k = jnp.concatenate([fwd, jnp.zeros((1, HY_ORDER, HY_W), F32), jnp.flip(bwd[1:], 0)], 0)
    k = k / jnp.sum(jnp.abs(k), axis=0, keepdims=True)
    return jnp.fft.rfft(k, axis=0)


def hyena_mixer(u, conv_w, conv_b, w1, b1, f1, w2, b2, f2, w3, d_skip):
    L = u.shape[1]
    up = jnp.pad(u, ((0, 0), (1, 1), (0, 0)))
    uc = conv_w[0] * up[:, :-2] + conv_w[1] * up[:, 1:-1] + conv_w[2] * up[:, 2:] + conv_b
    v, x1, x2 = jnp.split(uc, 3, axis=-1)
    kf = hyena_filter_spectra(L, w1, b1, f1, w2, b2, f2, w3)
    z = v
    for o, gate in enumerate((x1, x2)):
        zf = z.astype(F32)
        y = jnp.fft.irfft(jnp.fft.rfft(zf, n=2 * L, axis=1) * kf[:, o], n=2 * L, axis=1)[:, :L]
        z = gate * (y + zf * d_skip[o].astype(F32)).astype(u.dtype)
    return z


def neighbourhood_attention(q, k, v, rpb):
    B, L, _ = q.shape
    rows = L // GRID_W
    wr = min(NA_WIN_R, rows)
    wc = min(NA_WIN_C, GRID_W)
    shp = (B, rows, GRID_W, NA_HEADS, NA_HEAD_DIM)
    q, k, v = q.reshape(shp), k.reshape(shp), v.reshape(shp)
    cols = np.arange(GRID_W)
    col_idx = np.clip(cols - wc // 2, 0, GRID_W - wc)[:, None] + np.arange(wc)[None, :]
    dc_idx = col_idx - cols[:, None] + (NA_WIN_C - 1)
    rpb_c = rpb[:, :, dc_idx].astype(F32)
    scale = NA_HEAD_DIM ** -0.5

    def one_row(r):
        q_r = lax.dynamic_index_in_dim(q, r, axis=1, keepdims=False)
        rs = jnp.clip(r - wr // 2, 0, rows - wr)
        k_g = lax.dynamic_slice_in_dim(k, rs, wr, axis=1)[:, :, col_idx]
        v_g = lax.dynamic_slice_in_dim(v, rs, wr, axis=1)[:, :, col_idx]
        bias = rpb_c[:, rs + jnp.arange(wr) - r + (NA_WIN_R - 1)]
        s = jnp.einsum('bchd,bicjhd->bhcij', q_r, k_g).astype(F32) * scale
        s = s + jnp.transpose(bias, (0, 2, 1, 3))[None]
        p = jax.nn.softmax(s.reshape(B, NA_HEADS, GRID_W, wr * wc), -1).reshape(s.shape)
        return jnp.einsum('bhcij,bicjhd->bchd', p.astype(v.dtype), v_g)

    o = lax.map(one_row, jnp.arange(rows))
    return jnp.swapaxes(o, 0, 1).reshape(B, L, NA_W)


def latent_attention(cq, ckv, k_rope, q_norm_g, w_uq, kv_norm_g, w_ukv):
    B, L, _ = cq.shape
    q = (rms_norm(cq, q_norm_g) @ w_uq).reshape(B, L, MLA_HEADS, MLA_NOPE + MLA_ROPE)
    kv = (rms_norm(ckv, kv_norm_g) @ w_ukv).reshape(B, L, MLA_HEADS, MLA_NOPE + MLA_V)
    q_nope, q_rot = q[..., :MLA_NOPE], q[..., MLA_NOPE:]
    k_nope, v = kv[..., :MLA_NOPE], kv[..., MLA_NOPE:]
    pos = jnp.arange(L, dtype=F32)
    inv_freq = ROPE_BASE ** (-jnp.arange(0, MLA_ROPE, 2, dtype=F32) / MLA_ROPE)
    ang = pos[:, None] * inv_freq[None, :]
    cos, sin = jnp.cos(ang), jnp.sin(ang)
    q_rot = rope(q_rot, cos[:, None, :], sin[:, None, :])
    k_rot = rope(k_rope, cos, sin)
    scale = (MLA_NOPE + MLA_ROPE) ** -0.5
    nb = L // Q_BLOCK

    def to_blocks(t):
        return jnp.swapaxes(t.reshape(B, nb, Q_BLOCK, *t.shape[2:]), 0, 1)

    def attend(blk):
        qn, qr = blk
        s = jnp.einsum('bqhd,bkhd->bhqk', qn, k_nope) + jnp.einsum('bqhd,bkd->bhqk', qr, k_rot)
        p = jax.nn.softmax(s.astype(F32) * scale, -1).astype(v.dtype)
        return jnp.einsum('bhqk,bkhd->bqhd', p, v)

    o = lax.map(attend, (to_blocks(q_nope), to_blocks(q_rot)))
    return jnp.swapaxes(o, 0, 1).reshape(B, L, MLA_W)


def grouped_moe(h, router_w, router_b, w_gate, w_up, w_down):
    T, D = h.shape
    per = N_EXPERTS // N_GROUPS
    scores = jax.nn.sigmoid(h.astype(F32) @ router_w.astype(F32))
    sel = scores + router_b.astype(F32)
    grp = lax.top_k(sel.reshape(T, N_GROUPS, per), 2)[0].sum(-1)
    best = jnp.argmax(grp, -1)
    in_group = (jnp.arange(N_EXPERTS) // per)[None, :] == best[:, None]
    _, idx = lax.top_k(jnp.where(in_group, sel, -jnp.inf), TOP_K)
    gate = jnp.take_along_axis(scores, idx, -1)
    gate = gate / jnp.sum(gate, -1, keepdims=True)
    A = T * TOP_K
    flat_e = idx.reshape(A)
    flat_tok = jnp.arange(A, dtype=jnp.int32) // TOP_K
    flat_w = gate.reshape(A)
    order = jnp.argsort(flat_e)
    se, stok, sw = flat_e[order], flat_tok[order], flat_w[order]
    counts = jnp.bincount(flat_e, length=N_EXPERTS)
    starts = jnp.cumsum(counts) - counts
    padded = (counts + MOE_BLOCK - 1) // MOE_BLOCK * MOE_BLOCK
    pends = jnp.cumsum(padded)
    pstarts = pends - padded
    dest = pstarts[se] + jnp.arange(A, dtype=jnp.int32) - starts[se]
    P = A + N_EXPERTS * MOE_BLOCK
    nblk = P // MOE_BLOCK
    buf_tok = jnp.full((P,), T, jnp.int32).at[dest].set(stok.astype(jnp.int32))
    buf_w = jnp.zeros((P,), F32).at[dest].set(sw)
    blk_e = jnp.minimum(jnp.searchsorted(pends, jnp.arange(nblk) * MOE_BLOCK, side='right'), N_EXPERTS - 1)
    h_pad = jnp.concatenate([h, jnp.zeros((1, D), h.dtype)], 0)
    xb = h_pad[buf_tok].reshape(nblk, MOE_BLOCK, D)

    def expert_block(args):
        xblk, e = args
        return (jax.nn.silu(xblk @ w_gate[e]) * (xblk @ w_up[e])) @ w_down[e]

    yb = lax.map(expert_block, (xb, blk_e)).reshape(P, D)
    out = jax.ops.segment_sum(yb * buf_w[:, None].astype(yb.dtype), buf_tok, num_segments=T + 1)
    return out[:T]


def setup_inputs(seed: int = 0) -> dict:
    key = jax.random.key(seed)
    ks = iter(jax.random.split(key, 40))

    def nrm(shape, scale):
        return jax.random.normal(next(ks), shape, F32) * scale

    def gain(shape):
        return 1.0 + nrm(shape, 0.02)

    return {
        'x': nrm((BATCH, SEQ, D_MODEL), 1.0),
        'w_in': nrm((DEPTH, D_MODEL, IN_W), D_MODEL ** -0.5),
        'conv_w': nrm((DEPTH, 3, 3 * HY_W), 3 ** -0.5),
        'conv_b': nrm((DEPTH, 3 * HY_W), 0.02),
        'hy_w1': nrm((DEPTH, HY_EMB, HY_FFN), HY_EMB ** -0.5),
        'hy_b1': nrm((DEPTH, HY_FFN), 0.1),
        'hy_f1': gain((DEPTH, HY_FFN)),
        'hy_w2': nrm((DEPTH, HY_FFN, HY_FFN), HY_FFN ** -0.5),
        'hy_b2': nrm((DEPTH, HY_FFN), 0.1),
        'hy_f2': gain((DEPTH, HY_FFN)),
        'hy_w3': nrm((DEPTH, HY_FFN, 2 * HY_ORDER * HY_W), HY_FFN ** -0.5),
        'hy_d': nrm((DEPTH, HY_ORDER, HY_W), 0.1),
        'na_rpb': nrm((DEPTH, NA_HEADS, 2 * NA_WIN_R - 1, 2 * NA_WIN_C - 1), 0.05),
        'q_norm_g': gain((DEPTH, MLA_Q_RANK)),
        'w_uq': nrm((DEPTH, MLA_Q_RANK, MLA_HEADS * (MLA_NOPE + MLA_ROPE)), MLA_Q_RANK ** -0.5),
        'kv_norm_g': gain((DEPTH, MLA_KV_RANK)),
        'w_ukv': nrm((DEPTH, MLA_KV_RANK, MLA_HEADS * (MLA_NOPE + MLA_V)), MLA_KV_RANK ** -0.5),
        'grp_norm_g': gain((DEPTH, MIX_W)),
        'w_out': nrm((DEPTH, MIX_W, D_MODEL), MIX_W ** -0.5 * BETA),
        'ln1_g': gain((DEPTH, D_MODEL)),
        'ln1_b': nrm((DEPTH, D_MODEL), 0.02),
        'router_w': nrm((D_MODEL, N_EXPERTS), D_MODEL ** -0.5),
        'router_b': nrm((N_EXPERTS,), 0.01),
        'exp_w_gate': nrm((DEPTH, N_EXPERTS, D_MODEL, D_EXPERT), D_MODEL ** -0.5),
        'exp_w_up': nrm((DEPTH, N_EXPERTS, D_MODEL, D_EXPERT), D_MODEL ** -0.5),
        'exp_w_down': nrm((DEPTH, N_EXPERTS, D_EXPERT, D_MODEL), D_EXPERT ** -0.5 * BETA),
        'ln2_g': gain((DEPTH, D_MODEL)),
        'ln2_b': nrm((DEPTH, D_MODEL), 0.02),
    }


def reference(x, w_in, conv_w, conv_b, hy_w1, hy_b1, hy_f1, hy_w2, hy_b2, hy_f2, hy_w3, hy_d,
              na_rpb, q_norm_g, w_uq, kv_norm_g, w_ukv, grp_norm_g, w_out, ln1_g, ln1_b,
              router_w, router_b, exp_w_gate, exp_w_up, exp_w_down, ln2_g, ln2_b):
    B, L, D = x.shape
    splits = [3 * HY_W, 3 * HY_W + 3 * NA_W, 3 * HY_W + 3 * NA_W + MLA_Q_RANK,
              3 * HY_W + 3 * NA_W + MLA_Q_RANK + MLA_KV_RANK]
    for l in range(DEPTH):
        proj = x @ w_in[l]
        p_hy, p_na, p_cq, p_ckv, p_kr = jnp.split(proj, splits, axis=-1)
        y_hy = hyena_mixer(p_hy, conv_w[l], conv_b[l], hy_w1[l], hy_b1[l], hy_f1[l],
                           hy_w2[l], hy_b2[l], hy_f2[l], hy_w3[l], hy_d[l])
        q_na, k_na, v_na = jnp.split(p_na, 3, axis=-1)
        y_na = neighbourhood_attention(q_na, k_na, v_na, na_rpb[l])
        y_ml = latent_attention(p_cq, p_ckv, p_kr, q_norm_g[l], w_uq[l], kv_norm_g[l], w_ukv[l])
        g = grp_norm_g[l]
        mix = jnp.concatenate([rms_norm(y_hy, g[:HY_W]),
                               rms_norm(y_na, g[HY_W:HY_W + NA_W]),
                               rms_norm(y_ml, g[HY_W + NA_W:])], -1)
        x = layer_norm(ALPHA * x + mix @ w_out[l], ln1_g[l], ln1_b[l])
        ff = grouped_moe(x.reshape(B * L, D), router_w, router_b,
                         exp_w_gate[l], exp_w_up[l], exp_w_down[l]).reshape(B, L, D)
        x = layer_norm(ALPHA * x + ff, ln2_g[l], ln2_b[l])
    return x
```

```python
import functools
import math

import numpy as np
import jax
import jax.numpy as jnp
from jax import lax
from jax.experimental import pallas as pl
from jax.experimental.pallas import tpu as pltpu

F32 = jnp.float32
BF16 = jnp.bfloat16

DEPTH = 2
GRID_W = 64
HY_W = 512
HY_ORDER = 2
HY_BANDS = 16
HY_FFN = 64
HY_DECAY_TARGET = 1e-2
HY_FAST_DECAY = 0.3
HY_SLOW_DECAY = 1.5
NA_HEADS = 8
NA_HEAD_DIM = 64
NA_W = NA_HEADS * NA_HEAD_DIM
NA_WIN_R = 8
NA_WIN_C = 16
MLA_HEADS = 8
MLA_Q_RANK = 512
MLA_KV_RANK = 256
MLA_NOPE = 128
MLA_ROPE = 64
MLA_V = 128
MLA_W = MLA_HEADS * MLA_V
ROPE_BASE = 10000.0
N_EXPERTS = 64
N_GROUPS = 8
TOP_K = 2
D_EXPERT = 512
ALPHA = (2 * DEPTH) ** 0.25
EPS = 1e-5

FFT_N1 = 128
NA_BAND = 8
MOE_BLOCK = 256
NEG_BIG = -1e30
VMEM_LIMIT = 56 * 1024 * 1024

_NT = (((1,), (1,)), ((), ()))


def _cparams(sem):
    return pltpu.CompilerParams(dimension_semantics=sem, vmem_limit_bytes=VMEM_LIMIT)


def _mm_kernel(a_ref, b_ref, o_ref):
    a = a_ref[...].astype(BF16)
    b = b_ref[...].astype(BF16)
    o_ref[...] = jnp.dot(a, b, preferred_element_type=F32).astype(o_ref.dtype)


def matmul(a, b, *, bm, bn, out_dtype, name):
    M, K = a.shape
    _, N = b.shape
    bm, bn = min(bm, M), min(bn, N)
    assert M % bm == 0 and N % bn == 0
    return pl.pallas_call(
        _mm_kernel,
        out_shape=jax.ShapeDtypeStruct((M, N), out_dtype),
        grid=(N // bn, M // bm),
        in_specs=[pl.BlockSpec((bm, K), lambda j, i: (i, 0)),
                  pl.BlockSpec((K, bn), lambda j, i: (0, j))],
        out_specs=pl.BlockSpec((bm, bn), lambda j, i: (i, j)),
        compiler_params=_cparams(("parallel", "parallel")),
        name=name,
    )(a, b)


def _rms_mm_kernel(a_ref, g_ref, b_ref, *rest, n_rep):
    o_ref = rest[-1]
    a = a_ref[...]
    an = a * lax.rsqrt(jnp.mean(a * a, axis=-1, keepdims=True) + EPS) * g_ref[...]
    acc = jnp.dot(an.astype(BF16), b_ref[...], preferred_element_type=F32)
    if len(rest) == 2:
        e = rest[0][...]
        acc = acc * jnp.concatenate([e] * n_rep, axis=1)
    o_ref[...] = acc.astype(o_ref.dtype)


def rms_matmul(a, g, b, table, *, bm, seq_len, out_dtype, name):
    M, K = a.shape
    _, N = b.shape
    assert M % bm == 0 and seq_len % bm == 0
    in_specs = [pl.BlockSpec((bm, K), lambda i: (i, 0)),
                pl.BlockSpec((1, K), lambda i: (0, 0)),
                pl.BlockSpec((K, N), lambda i: (0, 0))]
    args = [a, g.reshape(1, K), b]
    n_rep = 1
    if table is not None:
        tw = table.shape[1]
        n_rep = N // tw
        nper = seq_len // bm
        in_specs.append(pl.BlockSpec((bm, tw), lambda i: (i % nper, 0)))
        args.append(table)
    return pl.pallas_call(
        functools.partial(_rms_mm_kernel, n_rep=n_rep),
        out_shape=jax.ShapeDtypeStruct((M, N), out_dtype),
        grid=(M // bm,),
        in_specs=in_specs,
        out_specs=pl.BlockSpec((bm, N), lambda i: (i, 0)),
        compiler_params=_cparams(("parallel",)),
        name=name,
    )(*args)


def _filter_mlp_kernel(feats_ref, w1_ref, b1_ref, f1_ref, w2_ref, b2_ref, f2_ref, w3_ref, dec_ref, o_ref):
    hp = lax.Precision.HIGHEST
    h = jnp.dot(feats_ref[...], w1_ref[...], preferred_element_type=F32, precision=hp)
    h = jnp.sin(f1_ref[...] * (h + b1_ref[...]))
    h = jnp.dot(h, w2_ref[...], preferred_element_type=F32, precision=hp)
    h = jnp.sin(f2_ref[...] * (h + b2_ref[...]))
    h = jnp.dot(h, w3_ref[...], preferred_element_type=F32, precision=hp)
    dec = dec_ref[...]
    o_ref[...] = h * jnp.concatenate([dec] * (h.shape[1] // dec.shape[1]), axis=1)


def filter_mlp(feats, w1, b1, f1, w2, b2, f2, w3, decay, *, bm):
    L, E = feats.shape
    Wd = w3.shape[1]
    full = lambda shp: pl.BlockSpec(shp, lambda i: (0, 0))
    return pl.pallas_call(
        _filter_mlp_kernel,
        out_shape=jax.ShapeDtypeStruct((L, Wd), F32),
        grid=(L // bm,),
        in_specs=[pl.BlockSpec((bm, E), lambda i: (i, 0)),
                  full(w1.shape), full((1, HY_FFN)), full((1, HY_FFN)),
                  full(w2.shape), full((1, HY_FFN)), full((1, HY_FFN)),
                  full(w3.shape), pl.BlockSpec((bm, HY_W), lambda i: (i, 0))],
        out_specs=pl.BlockSpec((bm, Wd), lambda i: (i, 0)),
        compiler_params=_cparams(("parallel",)),
        name="hy_filter_mlp",
    )(feats, w1, b1.reshape(1, -1), f1.reshape(1, -1), w2, b2.reshape(1, -1), f2.reshape(1, -1), w3, decay)


def _dft_tables(L):
    N = 2 * L
    N1 = FFT_N1
    N2 = N // N1
    h = N1 // 2
    k1 = jnp.arange(N1, dtype=jnp.int32)
    ang1 = ((k1[:, None] * k1[None, :]) % N1).astype(F32) * (2.0 * math.pi / N1)
    c1, s1 = jnp.cos(ang1), jnp.sin(ang1)
    m1 = jnp.concatenate([jnp.concatenate([c1[:, :h], s1[:, :h]], 1),
                          jnp.concatenate([-s1[:, :h], c1[:, :h]], 1)], 0)
    m1f = jnp.concatenate([c1, -s1], 0)
    ci, si = c1[:h, :], s1[:h, :]
    m4 = jnp.concatenate([jnp.concatenate([ci, -si], 1),
                          jnp.concatenate([si, ci], 1)], 0) * (1.0 / N)
    k2 = jnp.arange(N2, dtype=jnp.int32)
    kk = k1[:, None, None] + N1 * k2[None, :, None]
    ang2 = ((kk * k2[None, None, :]) % N).astype(F32) * (2.0 * math.pi / N)
    c2, s2 = jnp.cos(ang2), jnp.sin(ang2)
    gm = jnp.concatenate([jnp.concatenate([c2, s2], 2),
                          jnp.concatenate([-s2, c2], 2)], 1)
    gmt = jnp.swapaxes(gm, 1, 2)
    return (m1.astype(BF16), m1f.astype(BF16), m4.astype(BF16), gm.astype(BF16), gmt.astype(BF16))


def _stage2_kernel(gm_ref, a_ref, o_ref):
    n2 = a_ref.shape[1]
    a = a_ref[...].reshape(2 * n2, a_ref.shape[2])
    x = jnp.dot(gm_ref[...], a, preferred_element_type=F32)
    o_ref[...] = x.reshape(o_ref.shape)


def dft_stage2(gm, a):
    _, N1, N2, C = a.shape
    return pl.pallas_call(
        _stage2_kernel,
        out_shape=jax.ShapeDtypeStruct(a.shape, F32),
        grid=(N1,),
        in_specs=[pl.BlockSpec((None, 2 * N2, 2 * N2), lambda k: (k, 0, 0)),
                  pl.BlockSpec((2, None, N2, C), lambda k: (0, k, 0, 0))],
        out_specs=pl.BlockSpec((2, None, N2, C), lambda k: (0, k, 0, 0)),
        compiler_params=_cparams(("parallel",)),
        name="hy_filter_dft2",
    )(gm, a)


def _spectral_kernel(gm_ref, gmt_ref, a_ref, k_ref, o_ref):
    n2 = a_ref.shape[1]
    c = a_ref.shape[2]
    a = a_ref[...].reshape(2 * n2, c)
    x = jnp.dot(gm_ref[...], a, preferred_element_type=F32)
    xr, xi = x[:n2], x[n2:]
    kr, ki = k_ref[0], k_ref[1]
    y = jnp.concatenate([xr * kr - xi * ki, xr * ki + xi * kr], axis=0).astype(BF16)
    d = jnp.dot(gmt_ref[...], y, preferred_element_type=F32)
    o_ref[...] = d.reshape(o_ref.shape).astype(o_ref.dtype)


def spectral_stage(gm, gmt, a, kf, order):
    _, N1, N2, C = a.shape
    return pl.pallas_call(
        _spectral_kernel,
        out_shape=jax.ShapeDtypeStruct(a.shape, BF16),
        grid=(N1,),
        in_specs=[pl.BlockSpec((None, 2 * N2, 2 * N2), lambda k: (k, 0, 0)),
                  pl.BlockSpec((None, 2 * N2, 2 * N2), lambda k: (k, 0, 0)),
                  pl.BlockSpec((2, None, N2, C), lambda k: (0, k, 0, 0)),
                  pl.BlockSpec((2, None, N2, C), lambda k: (0, k, 0, order))],
        out_specs=pl.BlockSpec((2, None, N2, C), lambda k: (0, k, 0, 0)),
        compiler_params=_cparams(("parallel",)),
        name="hy_spectral",
    )(gm, gmt, a, kf)


def _inv_gate_kernel(m_ref, d_ref, z_ref, g_ref, s_ref, o_ref):
    y = jnp.dot(m_ref[...], d_ref[...], preferred_element_type=F32)
    o_ref[...] = g_ref[...] * (y + z_ref[...] * s_ref[...])


def inv_gate_stage(m4, d, z, gate, skip_row, *, bn):
    M, K = m4.shape
    _, N = d.shape
    bn = min(bn, N)
    return pl.pallas_call(
        _inv_gate_kernel,
        out_shape=jax.ShapeDtypeStruct((M, N), F32),
        grid=(N // bn,),
        in_specs=[pl.BlockSpec((M, K), lambda j: (0, 0)),
                  pl.BlockSpec((K, bn), lambda j: (0, j)),
                  pl.BlockSpec((M, bn), lambda j: (0, j)),
                  pl.BlockSpec((M, bn), lambda j: (0, j)),
                  pl.BlockSpec((1, bn), lambda j: (0, 0))],
        out_specs=pl.BlockSpec((M, bn), lambda j: (0, j)),
        compiler_params=_cparams(("parallel",)),
        name="hy_inv_gate",
    )(m4, d, z, gate, skip_row)


def hyena_filter_spectrum(L, w1, b1, f1, w2, b2, f2, w3, tables):
    _, m1f, _, gm, _ = tables
    N = 2 * L
    N2 = N // FFT_N1
    t = jnp.arange(L, dtype=F32) / L
    bands = jnp.linspace(1e-4, HY_BANDS - 1, HY_BANDS, dtype=F32)
    ang = (2.0 * math.pi) * t[:, None] * bands[None, :]
    feats = jnp.concatenate([t[:, None], jnp.cos(ang), -jnp.sin(ang)], -1)
    E = feats.shape[1]
    Ep = 128
    feats = jnp.pad(feats, ((0, 0), (0, Ep - E)))
    w1p = jnp.pad(w1, ((0, Ep - E), (0, 0)))
    deltas = jnp.abs(jnp.linspace(math.log(HY_DECAY_TARGET) / HY_SLOW_DECAY,
                                  math.log(HY_DECAY_TARGET) / HY_FAST_DECAY, HY_W, dtype=F32))
    decay = jnp.exp(-t[:, None] * deltas[None, :])
    h = filter_mlp(feats, w1p, b1, f1, w2, b2, f2, w3, decay, bm=min(1024, L))
    W = HY_ORDER * HY_W
    fwd, bwd = h[:, :W], h[:, W:]
    k = jnp.concatenate([fwd, jnp.zeros((1, W), F32), jnp.flip(bwd[1:], 0)], 0)
    k = k / jnp.sum(jnp.abs(k), axis=0, keepdims=True)
    a = matmul(m1f, k.reshape(FFT_N1, N2 * W), bm=2 * FFT_N1, bn=4096, out_dtype=BF16, name="hy_filter_dft1")
    return dft_stage2(gm, a.reshape(2, FFT_N1, N2, W))


def hyena_mixer(u, conv_w, conv_b, w1, b1, f1, w2, b2, f2, w3, d_skip, tables):
    B, L, _ = u.shape
    assert B == 2
    m1, _, m4, gm, gmt = tables
    N2 = 2 * L // FFT_N1
    up = jnp.pad(u, ((0, 0), (1, 1), (0, 0)))
    uc = conv_w[0] * up[:, :-2] + conv_w[1] * up[:, 1:-1] + conv_w[2] * up[:, 2:] + conv_b
    v, x1, x2 = jnp.split(uc, 3, axis=-1)
    kf = hyena_filter_spectrum(L, w1, b1, f1, w2, b2, f2, w3, tables)
    rows = FFT_N1
    cols = N2 * HY_W
    z = v.reshape(rows, cols)
    for o, gate in enumerate((x1, x2)):
        a = matmul(m1, z, bm=2 * FFT_N1, bn=4096, out_dtype=BF16, name="hy_dft1")
        d = spectral_stage(gm, gmt, a.reshape(2, FFT_N1, N2, HY_W), kf, o)
        skip_row = jnp.tile(d_skip[o].astype(F32), 4096 // HY_W).reshape(1, 4096)
        z = inv_gate_stage(m4, d.reshape(2 * FFT_N1, cols), z, gate.reshape(rows, cols), skip_row, bn=4096)
    return z.reshape(B, L, HY_W)


def _na_bias_table(rpb):
    wc = NA_WIN_C
    cols = np.arange(GRID_W)
    cs = np.clip(cols - wc // 2, 0, GRID_W - wc)
    allowed = (cols[None, :] >= cs[:, None]) & (cols[None, :] < cs[:, None] + wc)
    dc = np.clip(cols[None, :] - cols[:, None] + (NA_WIN_C - 1), 0, 2 * NA_WIN_C - 2)
    off = np.arange(NA_WIN_R)
    di = np.arange(NA_WIN_R)[None, :] - off[:, None] + (NA_WIN_R - 1)
    tab = rpb.astype(F32)[:, di[:, :, None, None], dc[None, None, :, :]]
    tab = jnp.where(jnp.asarray(allowed)[None, None, None], tab, NEG_BIG)
    tab = jnp.transpose(tab, (1, 0, 3, 2, 4))
    return tab.reshape(NA_WIN_R, NA_HEADS, GRID_W, NA_WIN_R * GRID_W)


def _na_kernel(q_ref, k0_ref, k1_ref, k2_ref, v0_ref, v1_ref, v2_ref, bias_ref, o_ref, kbuf, vbuf, *, rows):
    j = pl.program_id(1)
    nb = rows // NA_BAND
    bt = NA_BAND * GRID_W
    base = jnp.clip(j - 1, 0, nb - 3)
    for n, (kr, vr) in enumerate(((k0_ref, v0_ref), (k1_ref, v1_ref), (k2_ref, v2_ref))):
        kbuf[n * bt:(n + 1) * bt, :] = kr[...]
        vbuf[n * bt:(n + 1) * bt, :] = vr[...]
    lane = lax.broadcasted_iota(jnp.int32, (1, 2 * NA_HEAD_DIM), 1)
    first = lane < NA_HEAD_DIM
    scale = NA_HEAD_DIM ** -0.5
    nkeys = NA_WIN_R * GRID_W

    def row_body(t, carry):
        r = j * NA_BAND + t
        rs = jnp.clip(r - NA_WIN_R // 2, 0, rows - NA_WIN_R)
        ko = pl.multiple_of((rs - NA_BAND * base) * GRID_W, GRID_W)
        offi = r - rs
        qo = pl.multiple_of(t * GRID_W, GRID_W)
        for hp in range(NA_HEADS // 2):
            ls = slice(hp * 2 * NA_HEAD_DIM, (hp + 1) * 2 * NA_HEAD_DIM)
            q2 = q_ref[pl.ds(qo, GRID_W), ls]
            k2 = kbuf[pl.ds(ko, nkeys), ls]
            v2 = vbuf[pl.ds(ko, nkeys), ls]
            outs = []
            for s in range(2):
                qm = jnp.where(first if s == 0 else jnp.logical_not(first), q2, jnp.zeros_like(q2))
                sc = lax.dot_general(qm, k2, _NT, preferred_element_type=F32) * scale
                sc = sc + bias_ref[offi, 2 * hp + s]
                m = jnp.max(sc, axis=-1, keepdims=True)
                p = jnp.exp(sc - m)
                l = jnp.sum(p, axis=-1, keepdims=True)
                o = jnp.dot(p.astype(BF16), v2, preferred_element_type=F32)
                outs.append(o / l)
            o_ref[pl.ds(qo, GRID_W), ls] = jnp.where(first, outs[0], outs[1])
        return carry

    lax.fori_loop(0, NA_BAND, row_body, 0)


def neighbourhood_attention(qkv, rpb):
    B, L, _ = qkv.shape
    rows = L // GRID_W
    assert rows % NA_BAND == 0 and rows >= 3 * NA_BAND
    nb = rows // NA_BAND
    bt = NA_BAND * GRID_W
    bias = _na_bias_table(rpb)

    def kv_spec(col, n):
        return pl.BlockSpec((None, bt, NA_W), lambda b, j: (b, jnp.clip(j - 1, 0, nb - 3) + n, col))

    return pl.pallas_call(
        functools.partial(_na_kernel, rows=rows),
        out_shape=jax.ShapeDtypeStruct((B, L, NA_W), F32),
        grid=(B, nb),
        in_specs=[pl.BlockSpec((None, bt, NA_W), lambda b, j: (b, j, 0)),
                  kv_spec(1, 0), kv_spec(1, 1), kv_spec(1, 2),
                  kv_spec(2, 0), kv_spec(2, 1), kv_spec(2, 2),
                  pl.BlockSpec(bias.shape, lambda b, j: (0, 0, 0, 0))],
        out_specs=pl.BlockSpec((None, bt, NA_W), lambda b, j: (b, j, 0)),
        scratch_shapes=[pltpu.VMEM((3 * bt, NA_W), BF16), pltpu.VMEM((3 * bt, NA_W), BF16)],
        compiler_params=_cparams(("parallel", "parallel")),
        name="na_attention",
    )(qkv, qkv, qkv, qkv, qkv, qkv, qkv, bias)


def _flash_kernel(q_ref, kn_ref, kr_ref, v_ref, o_ref, m_ref, l_ref, acc_ref):
    ki = pl.program_id(3)

    @pl.when(ki == 0)
    def _():
        m_ref[...] = jnp.full(m_ref.shape, -jnp.inf, F32)
        l_ref[...] = jnp.zeros(l_ref.shape, F32)
        acc_ref[...] = jnp.zeros(acc_ref.shape, F32)

    k = jnp.concatenate([kn_ref[...], kr_ref[...]], axis=1)
    s = lax.dot_general(q_ref[...], k, _NT, preferred_element_type=F32)
    m_prev = m_ref[...]
    m_new = jnp.maximum(m_prev, jnp.max(s, axis=-1, keepdims=True))
    alpha = jnp.exp(m_prev - m_new)
    p = jnp.exp(s - m_new)
    l_ref[...] = alpha * l_ref[...] + jnp.sum(p, axis=-1, keepdims=True)
    acc_ref[...] = alpha * acc_ref[...] + jnp.dot(p.astype(BF16), v_ref[...], preferred_element_type=F32)
    m_ref[...] = m_new

    @pl.when(ki == pl.num_programs(3) - 1)
    def _():
        o_ref[...] = acc_ref[...] / l_ref[...]


def flash_attention(q, kv, krr, *, bq, bk):
    B, L, _ = q.shape
    H = MLA_HEADS
    bq, bk = min(bq, L), min(bk, L)
    dq = MLA_NOPE + 2 * MLA_ROPE
    return pl.pallas_call(
        _flash_kernel,
        out_shape=jax.ShapeDtypeStruct((B, L, H * MLA_V), F32),
        grid=(B, H, L // bq, L // bk),
        in_specs=[pl.BlockSpec((None, bq, dq), lambda b, h, i, k: (b, i, h)),
                  pl.BlockSpec((None, bk, MLA_NOPE), lambda b, h, i, k: (b, k, h)),
                  pl.BlockSpec((None, bk, 2 * MLA_ROPE), lambda b, h, i, k: (b, k, 0)),
                  pl.BlockSpec((None, bk, MLA_V), lambda b, h, i, k: (b, k, H + h))],
        out_specs=pl.BlockSpec((None, bq, MLA_V), lambda b, h, i, k: (b, i, h)),
        scratch_shapes=[pltpu.VMEM((bq, 1), F32), pltpu.VMEM((bq, 1), F32), pltpu.VMEM((bq, MLA_V), F32)],
        compiler_params=_cparams(("parallel", "parallel", "parallel", "arbitrary")),
        name="mla_flash",
    )(q, kv, krr, kv)


def latent_attention(c, q_norm_g, w_uq, kv_norm_g, w_ukv, B, L):
    H = MLA_HEADS
    cq = c[:, :MLA_Q_RANK]
    ckv = c[:, MLA_Q_RANK:MLA_Q_RANK + MLA_KV_RANK]
    k_rope = c[:, MLA_Q_RANK + MLA_KV_RANK:].reshape(B, L, MLA_ROPE)
    pos = jnp.arange(L, dtype=F32)
    inv_freq = ROPE_BASE ** (-jnp.arange(0, MLA_ROPE, 2, dtype=F32) / MLA_ROPE)
    ang = pos[:, None] * inv_freq[None, :]
    cos, sin = jnp.cos(ang), jnp.sin(ang)
    cos2 = jnp.concatenate([cos, cos], -1)
    sin2 = jnp.concatenate([sin, sin], -1)
    half = MLA_ROPE // 2
    wq = w_uq.reshape(MLA_Q_RANK, H, MLA_NOPE + MLA_ROPE)
    w_rot = wq[:, :, MLA_NOPE:]
    w_swap = jnp.concatenate([-w_rot[:, :, half:], w_rot[:, :, :half]], -1)
    wq2 = jnp.concatenate([wq, w_swap], -1).reshape(MLA_Q_RANK, H * (MLA_NOPE + 2 * MLA_ROPE)).astype(BF16)
    scale = (MLA_NOPE + MLA_ROPE) ** -0.5
    qtab = jnp.concatenate([jnp.ones((L, MLA_NOPE), F32), cos2, sin2], -1) * scale
    q = rms_matmul(cq, q_norm_g, wq2, qtab, bm=min(512, L), seq_len=L, out_dtype=BF16, name="mla_q_up")
    wkv = w_ukv.reshape(MLA_KV_RANK, H, MLA_NOPE + MLA_V)
    wkv2 = jnp.concatenate([wkv[:, :, :MLA_NOPE].reshape(MLA_KV_RANK, H * MLA_NOPE),
                            wkv[:, :, MLA_NOPE:].reshape(MLA_KV_RANK, H * MLA_V)], -1).astype(BF16)
    kv = rms_matmul(ckv, kv_norm_g, wkv2, None, bm=min(512, L), seq_len=L, out_dtype=BF16, name="mla_kv_up")
    x1, x2 = k_rope[..., :half], k_rope[..., half:]
    kr = jnp.concatenate([x1 * cos - x2 * sin, x1 * sin + x2 * cos], -1)
    krr = jnp.concatenate([kr, kr], -1).astype(BF16)
    return flash_attention(q.reshape(B, L, -1), kv.reshape(B, L, -1), krr, bq=1024, bk=512)


def _layer_norm(r, g, b):
    mu = jnp.mean(r, axis=-1, keepdims=True)
    d = r - mu
    var = jnp.mean(d * d, axis=-1, keepdims=True)
    return d * lax.rsqrt(var + EPS) * g + b


def _out_proj_kernel(hy_ref, na_ref, ml_ref, gg_ref, w_ref, x_ref, g_ref, b_ref, o_ref):
    def rms(y, g):
        return (y * lax.rsqrt(jnp.mean(y * y, axis=-1, keepdims=True) + EPS) * g).astype(BF16)

    gg = gg_ref[...]
    mix = jnp.concatenate([rms(hy_ref[...], gg[:, :HY_W]),
                           rms(na_ref[...], gg[:, HY_W:HY_W + NA_W]),
                           rms(ml_ref[...], gg[:, HY_W + NA_W:])], axis=1)
    acc = jnp.dot(mix, w_ref[...], preferred_element_type=F32)
    o_ref[...] = _layer_norm(ALPHA * x_ref[...] + acc, g_ref[...], b_ref[...])


def out_projection(y_hy, y_na, y_ml, grp_g, w_out_bf16, x, ln_g, ln_b, *, bm):
    T, D = x.shape
    row = lambda w: pl.BlockSpec((bm, w), lambda i: (i, 0))
    full = lambda shp: pl.BlockSpec(shp, lambda i: (0, 0))
    return pl.pallas_call(
        _out_proj_kernel,
        out_shape=jax.ShapeDtypeStruct((T, D), F32),
        grid=(T // bm,),
        in_specs=[row(HY_W), row(NA_W), row(MLA_W), full((1, D)), full(w_out_bf16.shape), row(D),
                  full((1, D)), full((1, D))],
        out_specs=row(D),
        compiler_params=_cparams(("parallel",)),
        name="out_proj_ln",
    )(y_hy, y_na, y_ml, grp_g.reshape(1, D), w_out_bf16, x, ln_g.reshape(1, D), ln_b.reshape(1, D))


def _router_kernel(x_ref, wh_ref, wl_ref, b_ref, idx_ref, gate_ref):
    x = x_ref[...]
    xh = x.astype(BF16)
    xl = (x - xh.astype(F32)).astype(BF16)
    wh, wl = wh_ref[...], wl_ref[...]
    lg = (lax.dot_general(wh, xh, _NT, preferred_element_type=F32)
          + lax.dot_general(wh, xl, _NT, preferred_element_type=F32)
          + lax.dot_general(wl, xh, _NT, preferred_element_type=F32))
    bm = lg.shape[1]
    per = N_EXPERTS // N_GROUPS
    sc = 1.0 / (1.0 + jnp.exp(-lg))
    sel = (sc + b_ref[...]).reshape(N_GROUPS, per, bm)
    sub = lax.broadcasted_iota(jnp.int32, sel.shape, 1).astype(F32)
    m1 = jnp.max(sel, axis=1, keepdims=True)
    i1 = jnp.min(jnp.where(sel == m1, sub, float(per)), axis=1, keepdims=True)
    rest = jnp.where(sub == i1, -jnp.inf, sel)
    m2 = jnp.max(rest, axis=1, keepdims=True)
    i2 = jnp.min(jnp.where(rest == m2, sub, float(per)), axis=1, keepdims=True)
    grp = (m1 + m2)[:, 0, :]
    gi = lax.broadcasted_iota(jnp.int32, grp.shape, 0).astype(F32)
    gmax = jnp.max(grp, axis=0, keepdims=True)
    best = jnp.min(jnp.where(grp == gmax, gi, float(N_GROUPS)), axis=0, keepdims=True)
    onb = gi == best
    e1 = jnp.sum(jnp.where(onb, gi * per + i1[:, 0, :], 0.0), axis=0, keepdims=True)
    e2 = jnp.sum(jnp.where(onb, gi * per + i2[:, 0, :], 0.0), axis=0, keepdims=True)
    ei = lax.broadcasted_iota(jnp.int32, sc.shape, 0).astype(F32)
    g1 = jnp.sum(jnp.where(ei == e1, sc, 0.0), axis=0, keepdims=True)
    g2 = jnp.sum(jnp.where(ei == e2, sc, 0.0), axis=0, keepdims=True)
    den = g1 + g2
    idx_ref[...] = jnp.concatenate([e1, e2], axis=0).astype(jnp.int32)
    gate_ref[...] = jnp.concatenate([g1 / den, g2 / den], axis=0)


def router(x, router_w, router_b, *, bm):
    T, D = x.shape
    wt = router_w.astype(F32).T
    wh = wt.astype(BF16)
    wl = (wt - wh.astype(F32)).astype(BF16)
    return pl.pallas_call(
        _router_kernel,
        out_shape=(jax.ShapeDtypeStruct((TOP_K, T), jnp.int32), jax.ShapeDtypeStruct((TOP_K, T), F32)),
        grid=(T // bm,),
        in_specs=[pl.BlockSpec((bm, D), lambda i: (i, 0)),
                  pl.BlockSpec((N_EXPERTS, D), lambda i: (0, 0)),
                  pl.BlockSpec((N_EXPERTS, D), lambda i: (0, 0)),
                  pl.BlockSpec((N_EXPERTS, 1), lambda i: (0, 0))],
        out_specs=(pl.BlockSpec((TOP_K, bm), lambda i: (0, i)), pl.BlockSpec((TOP_K, bm), lambda i: (0, i))),
        compiler_params=_cparams(("parallel",)),
        name="moe_router",
    )(x, wh, wl, router_b.astype(F32).reshape(N_EXPERTS, 1))


def _expert_kernel(blk_e_ref, nused_ref, x_ref, wg_ref, wu_ref, wd_ref, o_ref, wg_s, wu_s, wd_s):
    i = pl.program_id(0)
    prev = blk_e_ref[jnp.maximum(i - 1, 0)]
    new_expert = jnp.logical_or(i == 0, blk_e_ref[i] != prev)

    @pl.when(new_expert)
    def _():
        wg_s[...] = wg_ref[...].astype(BF16)
        wu_s[...] = wu_ref[...].astype(BF16)
        wd_s[...] = wd_ref[...].astype(BF16)

    @pl.when(i < nused_ref[0])
    def _():
        x = x_ref[...]
        g = jnp.dot(x, wg_s[...], preferred_element_type=F32)
        u = jnp.dot(x, wu_s[...], preferred_element_type=F32)
        h = (g * (1.0 / (1.0 + jnp.exp(-g))) * u).astype(BF16)
        o_ref[...] = jnp.dot(h, wd_s[...], preferred_element_type=F32)

    @pl.when(i >= nused_ref[0])
    def _():
        o_ref[...] = jnp.zeros(o_ref.shape, o_ref.dtype)


def expert_ffn(xb, blk_e, n_used, w_gate, w_up, w_down):
    P, D = xb.shape
    nblk = P // MOE_BLOCK
    xmap = lambda i, be, nu: (jnp.minimum(i, jnp.maximum(nu[0] - 1, 0)), 0)
    grid_spec = pltpu.PrefetchScalarGridSpec(
        num_scalar_prefetch=2,
        grid=(nblk,),
        in_specs=[pl.BlockSpec((MOE_BLOCK, D), xmap),
                  pl.BlockSpec((None, D, D_EXPERT), lambda i, be, nu: (be[i], 0, 0)),
                  pl.BlockSpec((None, D, D_EXPERT), lambda i, be, nu: (be[i], 0, 0)),
                  pl.BlockSpec((None, D_EXPERT, D), lambda i, be, nu: (be[i], 0, 0))],
        out_specs=pl.BlockSpec((MOE_BLOCK, D), lambda i, be, nu: (i, 0)),
        scratch_shapes=[pltpu.VMEM((D, D_EXPERT), BF16), pltpu.VMEM((D, D_EXPERT), BF16),
                        pltpu.VMEM((D_EXPERT, D), BF16)],
    )
    return pl.pallas_call(
        _expert_kernel,
        out_shape=jax.ShapeDtypeStruct((P, D), F32),
        grid_spec=grid_spec,
        compiler_params=_cparams(("arbitrary",)),
        name="moe_experts",
    )(blk_e, n_used, xb, w_gate, w_up, w_down)


def _combine_ln_kernel(x_ref, y0_ref, y1_ref, gt_ref, g_ref, b_ref, o_ref):
    gt = gt_ref[...]
    ff = y0_ref[...] * gt[:, 0:1] + y1_ref[...] * gt[:, 1:2]
    o_ref[...] = _layer_norm(ALPHA * x_ref[...] + ff, g_ref[...], b_ref[...])


def combine_ln(x, y0, y1, gate_t, ln_g, ln_b, *, bm):
    T, D = x.shape
    row = lambda w: pl.BlockSpec((bm, w), lambda i: (i, 0))
    full = lambda shp: pl.BlockSpec(shp, lambda i: (0, 0))
    return pl.pallas_call(
        _combine_ln_kernel,
        out_shape=jax.ShapeDtypeStruct((T, D), F32),
        grid=(T // bm,),
        in_specs=[row(D), row(D), row(D), row(TOP_K), full((1, D)), full((1, D))],
        out_specs=row(D),
        compiler_params=_cparams(("parallel",)),
        name="moe_combine_ln",
    )(x, y0, y1, gate_t, ln_g.reshape(1, D), ln_b.reshape(1, D))


def grouped_moe_ln(x, router_w, router_b, w_gate, w_up, w_down, ln_g, ln_b):
    T, D = x.shape
    idx, gate = router(x, router_w, router_b, bm=min(512, T))
    A = T * TOP_K
    flat_e = idx.T.reshape(A)
    order = jnp.argsort(flat_e)
    se = flat_e[order]
    counts = jnp.bincount(flat_e, length=N_EXPERTS)
    starts = jnp.cumsum(counts) - counts
    padded = (counts + MOE_BLOCK - 1) // MOE_BLOCK * MOE_BLOCK
    pends = jnp.cumsum(padded)
    pstarts = pends - padded
    dest = (pstarts[se] + jnp.arange(A, dtype=jnp.int32) - starts[se]).astype(jnp.int32)
    P = A + N_EXPERTS * MOE_BLOCK
    nblk = P // MOE_BLOCK
    buf_tok = jnp.zeros((P,), jnp.int32).at[dest].set((order // TOP_K).astype(jnp.int32))
    slot = jnp.zeros((A,), jnp.int32).at[order].set(dest).reshape(T, TOP_K)
    blk_e = jnp.minimum(jnp.searchsorted(pends, jnp.arange(nblk) * MOE_BLOCK, side='right'),
                        N_EXPERTS - 1).astype(jnp.int32)
    n_used = (pends[-1] // MOE_BLOCK).astype(jnp.int32).reshape(1)
    xb = x.astype(BF16)[buf_tok]
    yb = expert_ffn(xb, blk_e, n_used, w_gate, w_up, w_down)
    return combine_ln(x, yb[slot[:, 0]], yb[slot[:, 1]], gate.T, ln_g, ln_b, bm=min(256, T))


def kernel(x, w_in, conv_w, conv_b, hy_w1, hy_b1, hy_f1, hy_w2, hy_b2, hy_f2, hy_w3, hy_d,
           na_rpb, q_norm_g, w_uq, kv_norm_g, w_ukv, grp_norm_g, w_out, ln1_g, ln1_b,
           router_w, router_b, exp_w_gate, exp_w_up, exp_w_down, ln2_g, ln2_b):
    B, L, D = x.shape
    T = B * L
    tables = _dft_tables(L)
    s1 = 3 * HY_W
    s2 = s1 + 3 * NA_W
    xt = x.reshape(T, D)
    for l in range(DEPTH):
        w = w_in[l].astype(BF16)
        p_hy = matmul(xt, w[:, :s1], bm=1024, bn=512, out_dtype=F32, name="in_proj_hy")
        p_na = matmul(xt, w[:, s1:s2], bm=1024, bn=512, out_dtype=BF16, name="in_proj_na")
        p_c = matmul(xt, w[:, s2:], bm=1024, bn=w.shape[1] - s2, out_dtype=F32, name="in_proj_mla")
        y_hy = hyena_mixer(p_hy.reshape(B, L, s1), conv_w[l], conv_b[l], hy_w1[l], hy_b1[l], hy_f1[l],
                           hy_w2[l], hy_b2[l], hy_f2[l], hy_w3[l], hy_d[l], tables)
        y_na = neighbourhood_attention(p_na.reshape(B, L, 3 * NA_W), na_rpb[l])
        y_ml = latent_attention(p_c, q_norm_g[l], w_uq[l], kv_norm_g[l], w_ukv[l], B, L)
        xt = out_projection(y_hy.reshape(T, HY_W), y_na.reshape(T, NA_W), y_ml.reshape(T, MLA_W),
                            grp_norm_g[l], w_out[l].astype(BF16), xt, ln1_g[l], ln1_b[l], bm=256)
        xt = grouped_moe_ln(xt, router_w, router_b, exp_w_gate[l], exp_w_up[l], exp_w_down[l],
                            ln2_g[l], ln2_b[l])
    return xt.reshape(B, L, D)
```

```python
import functools
import math

import numpy as np
import jax
import jax.numpy as jnp
from jax import lax
from jax.experimental import pallas as pl
from jax.experimental.pallas import tpu as pltpu

F32 = jnp.float32
BF16 = jnp.bfloat16

DEPTH = 2
GRID_W = 64
HY_W = 512
HY_ORDER = 2
HY_BANDS = 16
HY_FFN = 64
HY_DECAY_TARGET = 1e-2
HY_FAST_DECAY = 0.3
HY_SLOW_DECAY = 1.5
NA_HEADS = 8
NA_HEAD_DIM = 64
NA_W = NA_HEADS * NA_HEAD_DIM
NA_WIN_R = 8
NA_WIN_C = 16
MLA_HEADS = 8
MLA_Q_RANK = 512
MLA_KV_RANK = 256
MLA_NOPE = 128
MLA_ROPE = 64
MLA_V = 128
MLA_W = MLA_HEADS * MLA_V
ROPE_BASE = 10000.0
N_EXPERTS = 64
N_GROUPS = 8
TOP_K = 2
D_EXPERT = 512
ALPHA = (2 * DEPTH) ** 0.25
EPS = 1e-5

FFT_N1 = 128
NA_BAND = 8
MOE_BLOCK = 256
NEG_BIG = -1e30
VMEM_LIMIT = 56 * 1024 * 1024

_NT = (((1,), (1,)), ((), ()))


def _cparams(sem):
    return pltpu.CompilerParams(dimension_semantics=sem, vmem_limit_bytes=VMEM_LIMIT)


def _mm_kernel(a_ref, b_ref, o_ref):
    a = a_ref[...].astype(BF16)
    b = b_ref[...].astype(BF16)
    o_ref[...] = jnp.dot(a, b, preferred_element_type=F32).astype(o_ref.dtype)


def matmul(a, b, *, bm, bn, out_dtype, name):
    M, K = a.shape
    _, N = b.shape
    bm, bn = min(bm, M), min(bn, N)
    assert M % bm == 0 and N % bn == 0
    return pl.pallas_call(
        _mm_kernel,
        out_shape=jax.ShapeDtypeStruct((M, N), out_dtype),
        grid=(N // bn, M // bm),
        in_specs=[pl.BlockSpec((bm, K), lambda j, i: (i, 0)),
                  pl.BlockSpec((K, bn), lambda j, i: (0, j))],
        out_specs=pl.BlockSpec((bm, bn), lambda j, i: (i, j)),
        compiler_params=_cparams(("parallel", "parallel")),
        name=name,
    )(a, b)


def _rms_mm_kernel(a_ref, g_ref, b_ref, *rest, n_rep):
    o_ref = rest[-1]
    a = a_ref[...]
    an = a * lax.rsqrt(jnp.mean(a * a, axis=-1, keepdims=True) + EPS) * g_ref[...]
    acc = jnp.dot(an.astype(BF16), b_ref[...], preferred_element_type=F32)
    if len(rest) == 2:
        e = rest[0][...]
        acc = acc * jnp.concatenate([e] * n_rep, axis=1)
    o_ref[...] = acc.astype(o_ref.dtype)


def rms_matmul(a, col_block, K, g, b, table, *, bm, seq_len, out_dtype, name):
    M = a.shape[0]
    _, N = b.shape
    assert M % bm == 0 and seq_len % bm == 0
    in_specs = [pl.BlockSpec((bm, K), lambda i: (i, col_block)),
                pl.BlockSpec((1, K), lambda i: (0, 0)),
                pl.BlockSpec((K, N), lambda i: (0, 0))]
    args = [a, g.reshape(1, K), b]
    n_rep = 1
    if table is not None:
        tw = table.shape[1]
        n_rep = N // tw
        nper = seq_len // bm
        in_specs.append(pl.BlockSpec((bm, tw), lambda i: (i % nper, 0)))
        args.append(table)
    return pl.pallas_call(
        functools.partial(_rms_mm_kernel, n_rep=n_rep),
        out_shape=jax.ShapeDtypeStruct((M, N), out_dtype),
        grid=(M // bm,),
        in_specs=in_specs,
        out_specs=pl.BlockSpec((bm, N), lambda i: (i, 0)),
        compiler_params=_cparams(("parallel",)),
        name=name,
    )(*args)


def _filter_mlp_kernel(feats_ref, w1_ref, b1_ref, f1_ref, w2_ref, b2_ref, f2_ref, w3_ref, dec_ref, o_ref, n_ref):
    hp = lax.Precision.HIGHEST
    h = jnp.dot(feats_ref[...], w1_ref[...], preferred_element_type=F32, precision=hp)
    h = jnp.sin(f1_ref[...] * (h + b1_ref[...]))
    h = jnp.dot(h, w2_ref[...], preferred_element_type=F32, precision=hp)
    h = jnp.sin(f2_ref[...] * (h + b2_ref[...]))
    h = jnp.dot(h, w3_ref[...], preferred_element_type=F32, precision=hp)
    dec = dec_ref[...]
    k = h * jnp.concatenate([dec] * (h.shape[1] // dec.shape[1]), axis=1)
    o_ref[...] = k

    @pl.when(pl.program_id(0) == 0)
    def _():
        n_ref[...] = jnp.zeros(n_ref.shape, F32)

    n_ref[...] += jnp.sum(jnp.abs(k), axis=0, keepdims=True)


def filter_mlp(feats, w1, b1, f1, w2, b2, f2, w3, decay, *, bm):
    N, E = feats.shape
    Wd = w3.shape[1] // 2
    nb = N // bm
    full = lambda shp: pl.BlockSpec(shp, lambda i: (0, 0))
    return pl.pallas_call(
        _filter_mlp_kernel,
        out_shape=(jax.ShapeDtypeStruct((N, Wd), F32), jax.ShapeDtypeStruct((1, Wd), F32)),
        grid=(nb,),
        in_specs=[pl.BlockSpec((bm, E), lambda i: (i, 0)),
                  full(w1.shape), full((1, HY_FFN)), full((1, HY_FFN)),
                  full(w2.shape), full((1, HY_FFN)), full((1, HY_FFN)),
                  pl.BlockSpec((HY_FFN, Wd), lambda i: (0, i // (nb // 2))),
                  pl.BlockSpec((bm, HY_W), lambda i: (i, 0))],
        out_specs=(pl.BlockSpec((bm, Wd), lambda i: (i, 0)), pl.BlockSpec((1, Wd), lambda i: (0, 0))),
        compiler_params=_cparams(("arbitrary",)),
        name="hy_filter_mlp",
    )(feats, w1, b1.reshape(1, -1), f1.reshape(1, -1), w2, b2.reshape(1, -1), f2.reshape(1, -1), w3, decay)


def _dft_tables(L):
    N = 2 * L
    N1 = FFT_N1
    N2 = N // N1
    h = N1 // 2
    k1 = jnp.arange(N1, dtype=jnp.int32)
    ang1 = ((k1[:, None] * k1[None, :]) % N1).astype(F32) * (2.0 * math.pi / N1)
    c1, s1 = jnp.cos(ang1), jnp.sin(ang1)
    m1 = jnp.concatenate([jnp.concatenate([c1[:, :h], s1[:, :h]], 1),
                          jnp.concatenate([-s1[:, :h], c1[:, :h]], 1)], 0)
    m1f = jnp.concatenate([c1, -s1], 0)
    ci, si = c1[:h, :], s1[:h, :]
    m4 = jnp.concatenate([jnp.concatenate([ci, -si], 1),
                          jnp.concatenate([si, ci], 1)], 0) * (1.0 / N)
    k2 = jnp.arange(N2, dtype=jnp.int32)
    kk = k1[:, None, None] + N1 * k2[None, :, None]
    ang2 = ((kk * k2[None, None, :]) % N).astype(F32) * (2.0 * math.pi / N)
    c2, s2 = jnp.cos(ang2), jnp.sin(ang2)
    gm = jnp.concatenate([jnp.concatenate([c2, s2], 2),
                          jnp.concatenate([-s2, c2], 2)], 1)
    gmt = jnp.swapaxes(gm, 1, 2)
    return (m1.astype(BF16), m1f.astype(BF16), m4.astype(BF16), gm.astype(BF16), gmt.astype(BF16))


def _stage2_kernel(gm_ref, a_ref, n_ref, o_ref):
    n2 = a_ref.shape[1]
    a = a_ref[...].reshape(2 * n2, a_ref.shape[2])
    x = jnp.dot(gm_ref[...], a, preferred_element_type=F32) / n_ref[...]
    o_ref[...] = x.reshape(o_ref.shape)


def dft_stage2(gm, a, norm):
    _, N1, N2, C = a.shape
    return pl.pallas_call(
        _stage2_kernel,
        out_shape=jax.ShapeDtypeStruct(a.shape, F32),
        grid=(N1,),
        in_specs=[pl.BlockSpec((None, 2 * N2, 2 * N2), lambda k: (k, 0, 0)),
                  pl.BlockSpec((2, None, N2, C), lambda k: (0, k, 0, 0)),
                  pl.BlockSpec((1, C), lambda k: (0, 0))],
        out_specs=pl.BlockSpec((2, None, N2, C), lambda k: (0, k, 0, 0)),
        compiler_params=_cparams(("parallel",)),
        name="hy_filter_dft2",
    )(gm, a, norm)


def _spectral_kernel(gm_ref, gmt_ref, a_ref, k_ref, o_ref):
    n2 = a_ref.shape[1]
    c = a_ref.shape[2]
    a = a_ref[...].reshape(2 * n2, c)
    x = jnp.dot(gm_ref[...], a, preferred_element_type=F32)
    xr, xi = x[:n2], x[n2:]
    kr, ki = k_ref[0], k_ref[1]
    y = jnp.concatenate([xr * kr - xi * ki, xr * ki + xi * kr], axis=0).astype(BF16)
    d = jnp.dot(gmt_ref[...], y, preferred_element_type=F32)
    o_ref[...] = d.reshape(o_ref.shape).astype(o_ref.dtype)


def spectral_stage(gm, gmt, a, kf, order):
    _, N1, N2, C = a.shape
    return pl.pallas_call(
        _spectral_kernel,
        out_shape=jax.ShapeDtypeStruct(a.shape, BF16),
        grid=(N1,),
        in_specs=[pl.BlockSpec((None, 2 * N2, 2 * N2), lambda k: (k, 0, 0)),
                  pl.BlockSpec((None, 2 * N2, 2 * N2), lambda k: (k, 0, 0)),
                  pl.BlockSpec((2, None, N2, C), lambda k: (0, k, 0, 0)),
                  pl.BlockSpec((2, None, N2, C), lambda k: (0, k, 0, order))],
        out_specs=pl.BlockSpec((2, None, N2, C), lambda k: (0, k, 0, 0)),
        compiler_params=_cparams(("parallel",)),
        name="hy_spectral",
    )(gm, gmt, a, kf)


def _inv_gate_kernel(m_ref, d_ref, z_ref, g_ref, s_ref, o_ref):
    y = jnp.dot(m_ref[...], d_ref[...], preferred_element_type=F32)
    o_ref[...] = g_ref[...] * (y + z_ref[...] * s_ref[...])


def inv_gate_stage(m4, d, z, gate, skip_row, *, bn):
    M, K = m4.shape
    _, N = d.shape
    bn = min(bn, N)
    return pl.pallas_call(
        _inv_gate_kernel,
        out_shape=jax.ShapeDtypeStruct((M, N), F32),
        grid=(N // bn,),
        in_specs=[pl.BlockSpec((M, K), lambda j: (0, 0)),
                  pl.BlockSpec((K, bn), lambda j: (0, j)),
                  pl.BlockSpec((M, bn), lambda j: (0, j)),
                  pl.BlockSpec((M, bn), lambda j: (0, j)),
                  pl.BlockSpec((1, bn), lambda j: (0, 0))],
        out_specs=pl.BlockSpec((M, bn), lambda j: (0, j)),
        compiler_params=_cparams(("parallel",)),
        name="hy_inv_gate",
    )(m4, d, z, gate, skip_row)


def hyena_filter_spectrum(L, w1, b1, f1, w2, b2, f2, w3, tables):
    _, m1f, _, gm, _ = tables
    N = 2 * L
    N2 = N // FFT_N1
    n = jnp.arange(N, dtype=jnp.int32)
    tidx = jnp.where(n < L, n, N - n)
    t = tidx.astype(F32) / L
    bands = jnp.linspace(1e-4, HY_BANDS - 1, HY_BANDS, dtype=F32)
    ang = (2.0 * math.pi) * t[:, None] * bands[None, :]
    feats = jnp.concatenate([t[:, None], jnp.cos(ang), -jnp.sin(ang)], -1)
    E = feats.shape[1]
    Ep = 128
    feats = jnp.pad(feats, ((0, 0), (0, Ep - E)))
    w1p = jnp.pad(w1, ((0, Ep - E), (0, 0)))
    deltas = jnp.abs(jnp.linspace(math.log(HY_DECAY_TARGET) / HY_SLOW_DECAY,
                                  math.log(HY_DECAY_TARGET) / HY_FAST_DECAY, HY_W, dtype=F32))
    decay = jnp.where((n == L)[:, None], 0.0, jnp.exp(-t[:, None] * deltas[None, :]))
    k, norm = filter_mlp(feats, w1p, b1, f1, w2, b2, f2, w3, decay, bm=min(1024, L))
    W = HY_ORDER * HY_W
    a = matmul(m1f, k.reshape(FFT_N1, N2 * W), bm=2 * FFT_N1, bn=4096, out_dtype=BF16, name="hy_filter_dft1")
    return dft_stage2(gm, a.reshape(2, FFT_N1, N2, W), norm)


def _conv3_kernel(u_ref, prev_ref, next_ref, w_ref, b_ref, o_ref):
    i = pl.program_id(2)
    x = u_ref[...]
    bt = x.shape[0]
    row = lax.broadcasted_iota(jnp.int32, x.shape, 0)
    sub = prev_ref.shape[0]
    prev_row = jnp.where(i > 0, prev_ref[sub - 1:sub, :], 0.0)
    next_row = jnp.where(i < pl.num_programs(2) - 1, next_ref[0:1, :], 0.0)
    x_dn = jnp.where(row == 0, prev_row, pltpu.roll(x, 1, axis=0))
    x_up = jnp.where(row == bt - 1, next_row, pltpu.roll(x, bt - 1, axis=0))
    w = w_ref[...]
    o_ref[...] = w[0:1] * x_dn + w[1:2] * x + w[2:3] * x_up + b_ref[...]


def short_conv(u, conv_w, conv_b, *, bt):
    B, L, C = u.shape
    G = C // HY_W
    sub = 8
    bt = min(bt, L)
    r = bt // sub
    return pl.pallas_call(
        _conv3_kernel,
        out_shape=jax.ShapeDtypeStruct((G, B, L, HY_W), F32),
        grid=(B, G, L // bt),
        in_specs=[pl.BlockSpec((None, bt, HY_W), lambda b, g, i: (b, i, g)),
                  pl.BlockSpec((None, sub, HY_W), lambda b, g, i: (b, jnp.maximum(i * r - 1, 0), g)),
                  pl.BlockSpec((None, sub, HY_W), lambda b, g, i: (b, jnp.minimum((i + 1) * r, L // sub - 1), g)),
                  pl.BlockSpec((3, HY_W), lambda b, g, i: (0, g)),
                  pl.BlockSpec((1, HY_W), lambda b, g, i: (0, g))],
        out_specs=pl.BlockSpec((None, None, bt, HY_W), lambda b, g, i: (g, b, i, 0)),
        compiler_params=_cparams(("parallel", "parallel", "parallel")),
        name="hy_short_conv",
    )(u, u, u, conv_w, conv_b.reshape(1, C))


def hyena_mixer(u, conv_w, conv_b, w1, b1, f1, w2, b2, f2, w3, d_skip, tables):
    B, L, _ = u.shape
    assert B == 2
    m1, _, m4, gm, gmt = tables
    N2 = 2 * L // FFT_N1
    v, x1, x2 = short_conv(u, conv_w, conv_b, bt=1024)
    kf = hyena_filter_spectrum(L, w1, b1, f1, w2, b2, f2, w3, tables)
    rows = FFT_N1
    cols = N2 * HY_W
    z = v.reshape(rows, cols)
    for o, gate in enumerate((x1, x2)):
        a = matmul(m1, z, bm=2 * FFT_N1, bn=4096, out_dtype=BF16, name="hy_dft1")
        d = spectral_stage(gm, gmt, a.reshape(2, FFT_N1, N2, HY_W), kf, o)
        skip_row = jnp.tile(d_skip[o].astype(F32), 4096 // HY_W).reshape(1, 4096)
        z = inv_gate_stage(m4, d.reshape(2 * FFT_N1, cols), z, gate.reshape(rows, cols), skip_row, bn=4096)
    return z.reshape(B, L, HY_W)


def _na_bias_table(rpb):
    wc = NA_WIN_C
    cols = np.arange(GRID_W)
    cs = np.clip(cols - wc // 2, 0, GRID_W - wc)
    allowed = (cols[None, :] >= cs[:, None]) & (cols[None, :] < cs[:, None] + wc)
    dc = np.clip(cols[None, :] - cols[:, None] + (NA_WIN_C - 1), 0, 2 * NA_WIN_C - 2)
    off = np.arange(NA_WIN_R)
    di = np.arange(NA_WIN_R)[None, :] - off[:, None] + (NA_WIN_R - 1)
    tab = rpb.astype(F32)[:, di[:, :, None, None], dc[None, None, :, :]]
    tab = jnp.where(jnp.asarray(allowed)[None, None, None], tab, NEG_BIG)
    tab = jnp.transpose(tab, (1, 0, 3, 2, 4))
    return tab.reshape(NA_WIN_R, NA_HEADS, GRID_W, NA_WIN_R * GRID_W)


def _na_kernel(q_ref, k0_ref, k1_ref, k2_ref, v0_ref, v1_ref, v2_ref, bias_ref, o_ref, kbuf, vbuf, *, rows):
    j = pl.program_id(1)
    nb = rows // NA_BAND
    bt = NA_BAND * GRID_W
    base = jnp.clip(j - 1, 0, nb - 3)
    for n, (kr, vr) in enumerate(((k0_ref, v0_ref), (k1_ref, v1_ref), (k2_ref, v2_ref))):
        kbuf[n * bt:(n + 1) * bt, :] = kr[...]
        vbuf[n * bt:(n + 1) * bt, :] = vr[...]
    lane = lax.broadcasted_iota(jnp.int32, (1, 2 * NA_HEAD_DIM), 1)
    first = lane < NA_HEAD_DIM
    scale = NA_HEAD_DIM ** -0.5
    nkeys = NA_WIN_R * GRID_W

    def row_body(t, carry):
        r = j * NA_BAND + t
        rs = jnp.clip(r - NA_WIN_R // 2, 0, rows - NA_WIN_R)
        ko = pl.multiple_of((rs - NA_BAND * base) * GRID_W, GRID_W)
        offi = r - rs
        qo = pl.multiple_of(t * GRID_W, GRID_W)
        for hp in range(NA_HEADS // 2):
            ls = slice(hp * 2 * NA_HEAD_DIM, (hp + 1) * 2 * NA_HEAD_DIM)
            q2 = q_ref[pl.ds(qo, GRID_W), ls]
            k2 = kbuf[pl.ds(ko, nkeys), ls]
            v2 = vbuf[pl.ds(ko, nkeys), ls]
            outs = []
            for s in range(2):
                qm = jnp.where(first if s == 0 else jnp.logical_not(first), q2, jnp.zeros_like(q2))
                sc = lax.dot_general(qm, k2, _NT, preferred_element_type=F32) * scale
                sc = sc + bias_ref[offi, 2 * hp + s]
                m = jnp.max(sc, axis=-1, keepdims=True)
                p = jnp.exp(sc - m)
                l = jnp.sum(p, axis=-1, keepdims=True)
                o = jnp.dot(p.astype(BF16), v2, preferred_element_type=F32)
                outs.append(o / l)
            o_ref[pl.ds(qo, GRID_W), ls] = jnp.where(first, outs[0], outs[1])
        return carry

    lax.fori_loop(0, NA_BAND, row_body, 0)


def neighbourhood_attention(qkv, rpb):
    B, L, _ = qkv.shape
    rows = L // GRID_W
    assert rows % NA_BAND == 0 and rows >= 3 * NA_BAND
    nb = rows // NA_BAND
    bt = NA_BAND * GRID_W
    bias = _na_bias_table(rpb)

    def kv_spec(col, n):
        return pl.BlockSpec((None, bt, NA_W), lambda b, j: (b, jnp.clip(j - 1, 0, nb - 3) + n, col))

    return pl.pallas_call(
        functools.partial(_na_kernel, rows=rows),
        out_shape=jax.ShapeDtypeStruct((B, L, NA_W), F32),
        grid=(B, nb),
        in_specs=[pl.BlockSpec((None, bt, NA_W), lambda b, j: (b, j, 0)),
                  kv_spec(1, 0), kv_spec(1, 1), kv_spec(1, 2),
                  kv_spec(2, 0), kv_spec(2, 1), kv_spec(2, 2),
                  pl.BlockSpec(bias.shape, lambda b, j: (0, 0, 0, 0))],
        out_specs=pl.BlockSpec((None, bt, NA_W), lambda b, j: (b, j, 0)),
        scratch_shapes=[pltpu.VMEM((3 * bt, NA_W), BF16), pltpu.VMEM((3 * bt, NA_W), BF16)],
        compiler_params=_cparams(("parallel", "parallel")),
        name="na_attention",
    )(qkv, qkv, qkv, qkv, qkv, qkv, qkv, bias)


def _flash_kernel(q_ref, kn_ref, kr_ref, v_ref, o_ref, m_ref, l_ref, acc_ref, *, bk, unroll):
    nk = kn_ref.shape[0] // bk
    q = q_ref[...]
    m_ref[...] = jnp.full(m_ref.shape, -jnp.inf, F32)
    l_ref[...] = jnp.zeros(l_ref.shape, F32)
    acc_ref[...] = jnp.zeros(acc_ref.shape, F32)

    def body(kc, carry):
        off = pl.multiple_of(kc * bk, bk)
        k = jnp.concatenate([kn_ref[pl.ds(off, bk), :], kr_ref[pl.ds(off, bk), :]], axis=1)
        s = lax.dot_general(q, k, _NT, preferred_element_type=F32)
        m_prev = m_ref[...]
        m_new = jnp.maximum(m_prev, jnp.max(s, axis=-1, keepdims=True))
        alpha = jnp.exp2(m_prev - m_new)
        p = jnp.exp2((s - jnp.concatenate([m_new] * (bk // MLA_V), axis=1)).astype(BF16))
        l_ref[...] = alpha * l_ref[...] + jnp.sum(p.astype(F32), axis=-1, keepdims=True)
        acc_ref[...] = alpha * acc_ref[...] + jnp.dot(p, v_ref[pl.ds(off, bk), :], preferred_element_type=F32)
        m_ref[...] = m_new
        return carry

    lax.fori_loop(0, nk, body, 0, unroll=unroll)
    o_ref[...] = acc_ref[...] / l_ref[...]


def flash_attention(q, kv, krr, *, bq, bk):
    B, L, _ = q.shape
    H = MLA_HEADS
    bq, bk = min(bq, L), min(bk, L)
    dq = MLA_NOPE + 2 * MLA_ROPE
    seq = lambda col: pl.BlockSpec((None, L, MLA_V), col)
    return pl.pallas_call(
        functools.partial(_flash_kernel, bk=bk, unroll=min(4, L // bk)),
        out_shape=jax.ShapeDtypeStruct((B, L, H * MLA_V), F32),
        grid=(B, H, L // bq),
        in_specs=[pl.BlockSpec((None, bq, dq), lambda b, h, i: (b, i, h)),
                  seq(lambda b, h, i: (b, 0, h)),
                  seq(lambda b, h, i: (b, 0, 0)),
                  seq(lambda b, h, i: (b, 0, H + h))],
        out_specs=pl.BlockSpec((None, bq, MLA_V), lambda b, h, i: (b, i, h)),
        scratch_shapes=[pltpu.VMEM((bq, MLA_V), F32), pltpu.VMEM((bq, MLA_V), F32), pltpu.VMEM((bq, MLA_V), F32)],
        compiler_params=_cparams(("parallel", "parallel", "arbitrary")),
        name="mla_flash",
    )(q, kv, krr, kv)


def latent_attention(c, q_norm_g, w_uq, kv_norm_g, w_ukv, B, L):
    H = MLA_HEADS
    k_rope = c[:, MLA_Q_RANK + MLA_KV_RANK:].reshape(B, L, MLA_ROPE)
    pos = jnp.arange(L, dtype=F32)
    inv_freq = ROPE_BASE ** (-jnp.arange(0, MLA_ROPE, 2, dtype=F32) / MLA_ROPE)
    ang = pos[:, None] * inv_freq[None, :]
    cos, sin = jnp.cos(ang), jnp.sin(ang)
    cos2 = jnp.concatenate([cos, cos], -1)
    sin2 = jnp.concatenate([sin, sin], -1)
    half = MLA_ROPE // 2
    wq = w_uq.reshape(MLA_Q_RANK, H, MLA_NOPE + MLA_ROPE)
    w_rot = wq[:, :, MLA_NOPE:]
    w_swap = jnp.concatenate([-w_rot[:, :, half:], w_rot[:, :, :half]], -1)
    wq2 = jnp.concatenate([wq, w_swap], -1).reshape(MLA_Q_RANK, H * (MLA_NOPE + 2 * MLA_ROPE)).astype(BF16)
    scale = (MLA_NOPE + MLA_ROPE) ** -0.5 * math.log2(math.e)
    qtab = jnp.concatenate([jnp.ones((L, MLA_NOPE), F32), cos2, sin2], -1) * scale
    q = rms_matmul(c, 0, MLA_Q_RANK, q_norm_g, wq2, qtab, bm=min(512, L), seq_len=L, out_dtype=BF16,
                   name="mla_q_up")
    wkv = w_ukv.reshape(MLA_KV_RANK, H, MLA_NOPE + MLA_V)
    wkv2 = jnp.concatenate([wkv[:, :, :MLA_NOPE].reshape(MLA_KV_RANK, H * MLA_NOPE),
                            wkv[:, :, MLA_NOPE:].reshape(MLA_KV_RANK, H * MLA_V)], -1).astype(BF16)
    kv = rms_matmul(c, MLA_Q_RANK // MLA_KV_RANK, MLA_KV_RANK, kv_norm_g, wkv2, None, bm=min(512, L), seq_len=L,
                    out_dtype=BF16, name="mla_kv_up")
    x1, x2 = k_rope[..., :half], k_rope[..., half:]
    kr = jnp.concatenate([x1 * cos - x2 * sin, x1 * sin + x2 * cos], -1)
    krr = jnp.concatenate([kr, kr], -1).astype(BF16)
    return flash_attention(q.reshape(B, L, -1), kv.reshape(B, L, -1), krr, bq=1024, bk=512)


def _layer_norm(r, g, b):
    mu = jnp.mean(r, axis=-1, keepdims=True)
    d = r - mu
    var = jnp.mean(d * d, axis=-1, keepdims=True)
    return d * lax.rsqrt(var + EPS) * g + b


def _out_proj_kernel(hy_ref, na_ref, ml_ref, gg_ref, w_ref, x_ref, g_ref, b_ref, o_ref):
    def rms(y, g):
        return (y * lax.rsqrt(jnp.mean(y * y, axis=-1, keepdims=True) + EPS) * g).astype(BF16)

    gg = gg_ref[...]
    mix = jnp.concatenate([rms(hy_ref[...], gg[:, :HY_W]),
                           rms(na_ref[...], gg[:, HY_W:HY_W + NA_W]),
                           rms(ml_ref[...], gg[:, HY_W + NA_W:])], axis=1)
    acc = jnp.dot(mix, w_ref[...], preferred_element_type=F32)
    o_ref[...] = _layer_norm(ALPHA * x_ref[...] + acc, g_ref[...], b_ref[...])


def out_projection(y_hy, y_na, y_ml, grp_g, w_out_bf16, x, ln_g, ln_b, *, bm):
    T, D = x.shape
    row = lambda w: pl.BlockSpec((bm, w), lambda i: (i, 0))
    full = lambda shp: pl.BlockSpec(shp, lambda i: (0, 0))
    return pl.pallas_call(
        _out_proj_kernel,
        out_shape=jax.ShapeDtypeStruct((T, D), F32),
        grid=(T // bm,),
        in_specs=[row(HY_W), row(NA_W), row(MLA_W), full((1, D)), full(w_out_bf16.shape), row(D),
                  full((1, D)), full((1, D))],
        out_specs=row(D),
        compiler_params=_cparams(("parallel",)),
        name="out_proj_ln",
    )(y_hy, y_na, y_ml, grp_g.reshape(1, D), w_out_bf16, x, ln_g.reshape(1, D), ln_b.reshape(1, D))


def _router_kernel(x_ref, wh_ref, wl_ref, b_ref, idx_ref, gate_ref):
    x = x_ref[...]
    xh = x.astype(BF16)
    xl = (x - xh.astype(F32)).astype(BF16)
    wh, wl = wh_ref[...], wl_ref[...]
    lg = (lax.dot_general(wh, xh, _NT, preferred_element_type=F32)
          + lax.dot_general(wh, xl, _NT, preferred_element_type=F32)
          + lax.dot_general(wl, xh, _NT, preferred_element_type=F32))
    bm = lg.shape[1]
    per = N_EXPERTS // N_GROUPS
    sc = 1.0 / (1.0 + jnp.exp(-lg))
    sel = (sc + b_ref[...]).reshape(N_GROUPS, per, bm)
    sub = lax.broadcasted_iota(jnp.int32, sel.shape, 1).astype(F32)
    m1 = jnp.max(sel, axis=1, keepdims=True)
    i1 = jnp.min(jnp.where(sel == m1, sub, float(per)), axis=1, keepdims=True)
    rest = jnp.where(sub == i1, -jnp.inf, sel)
    m2 = jnp.max(rest, axis=1, keepdims=True)
    i2 = jnp.min(jnp.where(rest == m2, sub, float(per)), axis=1, keepdims=True)
    grp = (m1 + m2)[:, 0, :]
    gi = lax.broadcasted_iota(jnp.int32, grp.shape, 0).astype(F32)
    gmax = jnp.max(grp, axis=0, keepdims=True)
    best = jnp.min(jnp.where(grp == gmax, gi, float(N_GROUPS)), axis=0, keepdims=True)
    onb = gi == best
    e1 = jnp.sum(jnp.where(onb, gi * per + i1[:, 0, :], 0.0), axis=0, keepdims=True)
    e2 = jnp.sum(jnp.where(onb, gi * per + i2[:, 0, :], 0.0), axis=0, keepdims=True)
    ei = lax.broadcasted_iota(jnp.int32, sc.shape, 0).astype(F32)
    g1 = jnp.sum(jnp.where(ei == e1, sc, 0.0), axis=0, keepdims=True)
    g2 = jnp.sum(jnp.where(ei == e2, sc, 0.0), axis=0, keepdims=True)
    den = g1 + g2
    idx_ref[...] = jnp.concatenate([e1, e2], axis=0).astype(jnp.int32)
    gate_ref[...] = jnp.concatenate([g1 / den, g2 / den], axis=0)


def router(x, router_w, router_b, *, bm):
    T, D = x.shape
    wt = router_w.astype(F32).T
    wh = wt.astype(BF16)
    wl = (wt - wh.astype(F32)).astype(BF16)
    return pl.pallas_call(
        _router_kernel,
        out_shape=(jax.ShapeDtypeStruct((TOP_K, T), jnp.int32), jax.ShapeDtypeStruct((TOP_K, T), F32)),
        grid=(T // bm,),
        in_specs=[pl.BlockSpec((bm, D), lambda i: (i, 0)),
                  pl.BlockSpec((N_EXPERTS, D), lambda i: (0, 0)),
                  pl.BlockSpec((N_EXPERTS, D), lambda i: (0, 0)),
                  pl.BlockSpec((N_EXPERTS, 1), lambda i: (0, 0))],
        out_specs=(pl.BlockSpec((TOP_K, bm), lambda i: (0, i)), pl.BlockSpec((TOP_K, bm), lambda i: (0, i))),
        compiler_params=_cparams(("parallel",)),
        name="moe_router",
    )(x, wh, wl, router_b.astype(F32).reshape(N_EXPERTS, 1))


def _expert_kernel(blk_e_ref, nused_ref, x_ref, wg_ref, wu_ref, wd_ref, o_ref, wg_s, wu_s, wd_s):
    i = pl.program_id(0)
    prev = blk_e_ref[jnp.maximum(i - 1, 0)]
    new_expert = jnp.logical_or(i == 0, blk_e_ref[i] != prev)

    @pl.when(new_expert)
    def _():
        wg_s[...] = wg_ref[...].astype(BF16)
        wu_s[...] = wu_ref[...].astype(BF16)
        wd_s[...] = wd_ref[...].astype(BF16)

    @pl.when(i < nused_ref[0])
    def _():
        x = x_ref[...]
        g = jnp.dot(x, wg_s[...], preferred_element_type=F32)
        u = jnp.dot(x, wu_s[...], preferred_element_type=F32)
        h = (g * (1.0 / (1.0 + jnp.exp(-g))) * u).astype(BF16)
        o_ref[...] = jnp.dot(h, wd_s[...], preferred_element_type=F32)

    @pl.when(i >= nused_ref[0])
    def _():
        o_ref[...] = jnp.zeros(o_ref.shape, o_ref.dtype)


def expert_ffn(xb, blk_e, n_used, w_gate, w_up, w_down):
    P, D = xb.shape
    nblk = P // MOE_BLOCK
    xmap = lambda i, be, nu: (jnp.minimum(i, jnp.maximum(nu[0] - 1, 0)), 0)
    grid_spec = pltpu.PrefetchScalarGridSpec(
        num_scalar_prefetch=2,
        grid=(nblk,),
        in_specs=[pl.BlockSpec((MOE_BLOCK, D), xmap),
                  pl.BlockSpec((None, D, D_EXPERT), lambda i, be, nu: (be[i], 0, 0)),
                  pl.BlockSpec((None, D, D_EXPERT), lambda i, be, nu: (be[i], 0, 0)),
                  pl.BlockSpec((None, D_EXPERT, D), lambda i, be, nu: (be[i], 0, 0))],
        out_specs=pl.BlockSpec((MOE_BLOCK, D), lambda i, be, nu: (i, 0)),
        scratch_shapes=[pltpu.VMEM((D, D_EXPERT), BF16), pltpu.VMEM((D, D_EXPERT), BF16),
                        pltpu.VMEM((D_EXPERT, D), BF16)],
    )
    return pl.pallas_call(
        _expert_kernel,
        out_shape=jax.ShapeDtypeStruct((P, D), F32),
        grid_spec=grid_spec,
        compiler_params=_cparams(("arbitrary",)),
        name="moe_experts",
    )(blk_e, n_used, xb, w_gate, w_up, w_down)


def _combine_ln_kernel(x_ref, y0_ref, y1_ref, gt_ref, g_ref, b_ref, o_ref):
    gt = gt_ref[...]
    ff = y0_ref[...] * gt[:, 0:1] + y1_ref[...] * gt[:, 1:2]
    o_ref[...] = _layer_norm(ALPHA * x_ref[...] + ff, g_ref[...], b_ref[...])


def combine_ln(x, y0, y1, gate_t, ln_g, ln_b, *, bm):
    T, D = x.shape
    row = lambda w: pl.BlockSpec((bm, w), lambda i: (i, 0))
    full = lambda shp: pl.BlockSpec(shp, lambda i: (0, 0))
    return pl.pallas_call(
        _combine_ln_kernel,
        out_shape=jax.ShapeDtypeStruct((T, D), F32),
        grid=(T // bm,),
        in_specs=[row(D), row(D), row(D), row(TOP_K), full((1, D)), full((1, D))],
        out_specs=row(D),
        compiler_params=_cparams(("parallel",)),
        name="moe_combine_ln",
    )(x, y0, y1, gate_t, ln_g.reshape(1, D), ln_b.reshape(1, D))


def grouped_moe_ln(x, router_w, router_b, w_gate, w_up, w_down, ln_g, ln_b):
    T, D = x.shape
    idx, gate = router(x, router_w, router_b, bm=min(512, T))
    A = T * TOP_K
    flat_e = idx.T.reshape(A)
    order = jnp.argsort(flat_e)
    se = flat_e[order]
    counts = jnp.bincount(flat_e, length=N_EXPERTS)
    starts = jnp.cumsum(counts) - counts
    padded = (counts + MOE_BLOCK - 1) // MOE_BLOCK * MOE_BLOCK
    pends = jnp.cumsum(padded)
    pstarts = pends - padded
    dest = (pstarts[se] + jnp.arange(A, dtype=jnp.int32) - starts[se]).astype(jnp.int32)
    P = A + N_EXPERTS * MOE_BLOCK
    nblk = P // MOE_BLOCK
    buf_tok = jnp.zeros((P,), jnp.int32).at[dest].set((order // TOP_K).astype(jnp.int32))
    slot = jnp.zeros((A,), jnp.int32).at[order].set(dest).reshape(T, TOP_K)
    blk_e = jnp.minimum(jnp.searchsorted(pends, jnp.arange(nblk) * MOE_BLOCK, side='right'),
                        N_EXPERTS - 1).astype(jnp.int32)
    n_used = (pends[-1] // MOE_BLOCK).astype(jnp.int32).reshape(1)
    xb = x.astype(BF16)[buf_tok]
    yb = expert_ffn(xb, blk_e, n_used, w_gate, w_up, w_down)
    return combine_ln(x, yb[slot[:, 0]], yb[slot[:, 1]], gate.T, ln_g, ln_b, bm=min(256, T))


def kernel(x, w_in, conv_w, conv_b, hy_w1, hy_b1, hy_f1, hy_w2, hy_b2, hy_f2, hy_w3, hy_d,
           na_rpb, q_norm_g, w_uq, kv_norm_g, w_ukv, grp_norm_g, w_out, ln1_g, ln1_b,
           router_w, router_b, exp_w_gate, exp_w_up, exp_w_down, ln2_g, ln2_b):
    B, L, D = x.shape
    T = B * L
    tables = _dft_tables(L)
    s1 = 3 * HY_W
    s2 = s1 + 3 * NA_W
    xt = x.reshape(T, D)
    for l in range(DEPTH):
        w = w_in[l].astype(BF16)
        p_hy = matmul(xt, w[:, :s1], bm=1024, bn=512, out_dtype=F32, name="in_proj_hy")
        p_na = matmul(xt, w[:, s1:s2], bm=1024, bn=512, out_dtype=BF16, name="in_proj_na")
        p_c = matmul(xt, w[:, s2:], bm=1024, bn=w.shape[1] - s2, out_dtype=F32, name="in_proj_mla")
        y_hy = hyena_mixer(p_hy.reshape(B, L, s1), conv_w[l], conv_b[l], hy_w1[l], hy_b1[l], hy_f1[l],
                           hy_w2[l], hy_b2[l], hy_f2[l], hy_w3[l], hy_d[l], tables)
        y_na = neighbourhood_attention(p_na.reshape(B, L, 3 * NA_W), na_rpb[l])
        y_ml = latent_attention(p_c, q_norm_g[l], w_uq[l], kv_norm_g[l], w_ukv[l], B, L)
        xt = out_projection(y_hy.reshape(T, HY_W), y_na.reshape(T, NA_W), y_ml.reshape(T, MLA_W),
                            grp_norm_g[l], w_out[l].astype(BF16), xt, ln1_g[l], ln1_b[l], bm=256)
        xt = grouped_moe_ln(xt, router_w, router_b, exp_w_gate[l], exp_w_up[l], exp_w_down[l],
                            ln2_g[l], ln2_b[l])
    return xt.reshape(B, L, D)
```

```python
import functools
import math

import numpy as np
import jax
import jax.numpy as jnp
from jax import lax
from jax.experimental import pallas as pl
from jax.experimental.pallas import tpu as pltpu

F32 = jnp.float32
BF16 = jnp.bfloat16

DEPTH = 2
GRID_W = 64
HY_W = 512
HY_ORDER = 2
HY_BANDS = 16
HY_FFN = 64
HY_DECAY_TARGET = 1e-2
HY_FAST_DECAY = 0.3
HY_SLOW_DECAY = 1.5
NA_HEADS = 8
NA_HEAD_DIM = 64
NA_W = NA_HEADS * NA_HEAD_DIM
NA_WIN_R = 8
NA_WIN_C = 16
MLA_HEADS = 8
MLA_Q_RANK = 512
MLA_KV_RANK = 256
MLA_NOPE = 128
MLA_ROPE = 64
MLA_V = 128
MLA_W = MLA_HEADS * MLA_V
ROPE_BASE = 10000.0
N_EXPERTS = 64
N_GROUPS = 8
TOP_K = 2
D_EXPERT = 512
ALPHA = (2 * DEPTH) ** 0.25
EPS = 1e-5

FFT_N1 = 128
NA_BAND = 8
MOE_BLOCK = 256
ROW_SUB = 16
NEG_BIG = -1e30
VMEM_LIMIT = 56 * 1024 * 1024

_NT = (((1,), (1,)), ((), ()))


def _cparams(sem):
    return pltpu.CompilerParams(dimension_semantics=sem, vmem_limit_bytes=VMEM_LIMIT)


def _mm_kernel(a_ref, b_ref, o_ref):
    a = a_ref[...].astype(BF16)
    b = b_ref[...].astype(BF16)
    o_ref[...] = jnp.dot(a, b, preferred_element_type=F32).astype(o_ref.dtype)


def matmul(a, b, *, bm, bn, out_dtype, name):
    M, K = a.shape
    _, N = b.shape
    bm, bn = min(bm, M), min(bn, N)
    assert M % bm == 0 and N % bn == 0
    return pl.pallas_call(
        _mm_kernel,
        out_shape=jax.ShapeDtypeStruct((M, N), out_dtype),
        grid=(N // bn, M // bm),
        in_specs=[pl.BlockSpec((bm, K), lambda j, i: (i, 0)),
                  pl.BlockSpec((K, bn), lambda j, i: (0, j))],
        out_specs=pl.BlockSpec((bm, bn), lambda j, i: (i, j)),
        compiler_params=_cparams(("parallel", "parallel")),
        name=name,
    )(a, b)


def _rms_mm_kernel(a_ref, g_ref, b_ref, *rest, n_rep):
    o_ref = rest[-1]
    a = a_ref[...]
    an = a * lax.rsqrt(jnp.mean(a * a, axis=-1, keepdims=True) + EPS) * g_ref[...]
    acc = jnp.dot(an.astype(BF16), b_ref[...], preferred_element_type=F32)
    if len(rest) == 2:
        e = rest[0][...]
        acc = acc * jnp.concatenate([e] * n_rep, axis=1)
    o_ref[...] = acc.astype(o_ref.dtype)


def rms_matmul(a, col_block, K, g, b, table, *, bm, seq_len, out_dtype, name):
    M = a.shape[0]
    _, N = b.shape
    assert M % bm == 0 and seq_len % bm == 0
    in_specs = [pl.BlockSpec((bm, K), lambda i: (i, col_block)),
                pl.BlockSpec((1, K), lambda i: (0, 0)),
                pl.BlockSpec((K, N), lambda i: (0, 0))]
    args = [a, g.reshape(1, K), b]
    n_rep = 1
    if table is not None:
        tw = table.shape[1]
        n_rep = N // tw
        nper = seq_len // bm
        in_specs.append(pl.BlockSpec((bm, tw), lambda i: (i % nper, 0)))
        args.append(table)
    return pl.pallas_call(
        functools.partial(_rms_mm_kernel, n_rep=n_rep),
        out_shape=jax.ShapeDtypeStruct((M, N), out_dtype),
        grid=(M // bm,),
        in_specs=in_specs,
        out_specs=pl.BlockSpec((bm, N), lambda i: (i, 0)),
        compiler_params=_cparams(("parallel",)),
        name=name,
    )(*args)


def _filter_mlp_kernel(feats_ref, w1_ref, b1_ref, f1_ref, w2_ref, b2_ref, f2_ref, w3_ref, dec_ref, o_ref, n_ref):
    hp = lax.Precision.HIGHEST
    h = jnp.dot(feats_ref[...], w1_ref[...], preferred_element_type=F32, precision=hp)
    h = jnp.sin(f1_ref[...] * (h + b1_ref[...]))
    h = jnp.dot(h, w2_ref[...], preferred_element_type=F32, precision=hp)
    h = jnp.sin(f2_ref[...] * (h + b2_ref[...]))
    h = jnp.dot(h, w3_ref[...], preferred_element_type=F32, precision=hp)
    dec = dec_ref[...]
    k = h * jnp.concatenate([dec] * (h.shape[1] // dec.shape[1]), axis=1)
    o_ref[...] = k

    @pl.when(pl.program_id(0) == 0)
    def _():
        n_ref[...] = jnp.zeros(n_ref.shape, F32)

    n_ref[...] += jnp.sum(jnp.abs(k), axis=0, keepdims=True)


def filter_mlp(feats, w1, b1, f1, w2, b2, f2, w3, decay, *, bm):
    N, E = feats.shape
    Wd = w3.shape[1] // 2
    nb = N // bm
    full = lambda shp: pl.BlockSpec(shp, lambda i: (0, 0))
    return pl.pallas_call(
        _filter_mlp_kernel,
        out_shape=(jax.ShapeDtypeStruct((N, Wd), F32), jax.ShapeDtypeStruct((1, Wd), F32)),
        grid=(nb,),
        in_specs=[pl.BlockSpec((bm, E), lambda i: (i, 0)),
                  full(w1.shape), full((1, HY_FFN)), full((1, HY_FFN)),
                  full(w2.shape), full((1, HY_FFN)), full((1, HY_FFN)),
                  pl.BlockSpec((HY_FFN, Wd), lambda i: (0, i // (nb // 2))),
                  pl.BlockSpec((bm, HY_W), lambda i: (i, 0))],
        out_specs=(pl.BlockSpec((bm, Wd), lambda i: (i, 0)), pl.BlockSpec((1, Wd), lambda i: (0, 0))),
        compiler_params=_cparams(("arbitrary",)),
        name="hy_filter_mlp",
    )(feats, w1, b1.reshape(1, -1), f1.reshape(1, -1), w2, b2.reshape(1, -1), f2.reshape(1, -1), w3, decay)


def _dft_tables(L):
    N = 2 * L
    N1 = FFT_N1
    N2 = N // N1
    h = N1 // 2
    k1 = jnp.arange(N1, dtype=jnp.int32)
    ang1 = ((k1[:, None] * k1[None, :]) % N1).astype(F32) * (2.0 * math.pi / N1)
    c1, s1 = jnp.cos(ang1), jnp.sin(ang1)
    m1 = jnp.concatenate([jnp.concatenate([c1[:, :h], s1[:, :h]], 1),
                          jnp.concatenate([-s1[:, :h], c1[:, :h]], 1)], 0)
    m1f = jnp.concatenate([c1, -s1], 0)
    ci, si = c1[:h, :], s1[:h, :]
    m4 = jnp.concatenate([jnp.concatenate([ci, -si], 1),
                          jnp.concatenate([si, ci], 1)], 0) * (1.0 / N)
    k2 = jnp.arange(N2, dtype=jnp.int32)
    kk = k1[:, None, None] + N1 * k2[None, :, None]
    ang2 = ((kk * k2[None, None, :]) % N).astype(F32) * (2.0 * math.pi / N)
    c2, s2 = jnp.cos(ang2), jnp.sin(ang2)
    gm = jnp.concatenate([jnp.concatenate([c2, s2], 2),
                          jnp.concatenate([-s2, c2], 2)], 1)
    gmt = jnp.swapaxes(gm, 1, 2)
    return (m1.astype(BF16), m1f.astype(BF16), m4.astype(BF16), gm.astype(BF16), gmt.astype(BF16))


def _stage2_kernel(gm_ref, a_ref, n_ref, o_ref):
    n2 = a_ref.shape[1]
    a = a_ref[...].reshape(2 * n2, a_ref.shape[2])
    x = jnp.dot(gm_ref[...], a, preferred_element_type=F32) / n_ref[...]
    o_ref[...] = x.reshape(o_ref.shape)


def dft_stage2(gm, a, norm):
    _, N1, N2, C = a.shape
    return pl.pallas_call(
        _stage2_kernel,
        out_shape=jax.ShapeDtypeStruct(a.shape, F32),
        grid=(N1,),
        in_specs=[pl.BlockSpec((None, 2 * N2, 2 * N2), lambda k: (k, 0, 0)),
                  pl.BlockSpec((2, None, N2, C), lambda k: (0, k, 0, 0)),
                  pl.BlockSpec((1, C), lambda k: (0, 0))],
        out_specs=pl.BlockSpec((2, None, N2, C), lambda k: (0, k, 0, 0)),
        compiler_params=_cparams(("parallel",)),
        name="hy_filter_dft2",
    )(gm, a, norm)


def _spectral_kernel(gm_ref, gmt_ref, a_ref, k_ref, o_ref):
    n2 = a_ref.shape[1]
    c = a_ref.shape[2]
    a = a_ref[...].reshape(2 * n2, c)
    x = jnp.dot(gm_ref[...], a, preferred_element_type=F32)
    xr, xi = x[:n2], x[n2:]
    kr, ki = k_ref[0], k_ref[1]
    y = jnp.concatenate([xr * kr - xi * ki, xr * ki + xi * kr], axis=0).astype(BF16)
    d = jnp.dot(gmt_ref[...], y, preferred_element_type=F32)
    o_ref[...] = d.reshape(o_ref.shape).astype(o_ref.dtype)


def spectral_stage(gm, gmt, a, kf, order):
    _, N1, N2, C = a.shape
    return pl.pallas_call(
        _spectral_kernel,
        out_shape=jax.ShapeDtypeStruct(a.shape, BF16),
        grid=(N1,),
        in_specs=[pl.BlockSpec((None, 2 * N2, 2 * N2), lambda k: (k, 0, 0)),
                  pl.BlockSpec((None, 2 * N2, 2 * N2), lambda k: (k, 0, 0)),
                  pl.BlockSpec((2, None, N2, C), lambda k: (0, k, 0, 0)),
                  pl.BlockSpec((2, None, N2, C), lambda k: (0, k, 0, order))],
        out_specs=pl.BlockSpec((2, None, N2, C), lambda k: (0, k, 0, 0)),
        compiler_params=_cparams(("parallel",)),
        name="hy_spectral",
    )(gm, gmt, a, kf)


def _inv_gate_kernel(m_ref, d_ref, z_ref, g_ref, s_ref, o_ref):
    y = jnp.dot(m_ref[...], d_ref[...], preferred_element_type=F32)
    o_ref[...] = g_ref[...] * (y + z_ref[...] * s_ref[...])


def inv_gate_stage(m4, d, z, gate, skip_row, *, bn):
    M, K = m4.shape
    _, N = d.shape
    bn = min(bn, N)
    return pl.pallas_call(
        _inv_gate_kernel,
        out_shape=jax.ShapeDtypeStruct((M, N), F32),
        grid=(N // bn,),
        in_specs=[pl.BlockSpec((M, K), lambda j: (0, 0)),
                  pl.BlockSpec((K, bn), lambda j: (0, j)),
                  pl.BlockSpec((M, bn), lambda j: (0, j)),
                  pl.BlockSpec((M, bn), lambda j: (0, j)),
                  pl.BlockSpec((1, bn), lambda j: (0, 0))],
        out_specs=pl.BlockSpec((M, bn), lambda j: (0, j)),
        compiler_params=_cparams(("parallel",)),
        name="hy_inv_gate",
    )(m4, d, z, gate, skip_row)


def hyena_filter_spectrum(L, w1, b1, f1, w2, b2, f2, w3, tables):
    _, m1f, _, gm, _ = tables
    N = 2 * L
    N2 = N // FFT_N1
    n = jnp.arange(N, dtype=jnp.int32)
    tidx = jnp.where(n < L, n, N - n)
    t = tidx.astype(F32) / L
    bands = jnp.linspace(1e-4, HY_BANDS - 1, HY_BANDS, dtype=F32)
    ang = (2.0 * math.pi) * t[:, None] * bands[None, :]
    feats = jnp.concatenate([t[:, None], jnp.cos(ang), -jnp.sin(ang)], -1)
    E = feats.shape[1]
    Ep = 128
    feats = jnp.pad(feats, ((0, 0), (0, Ep - E)))
    w1p = jnp.pad(w1, ((0, Ep - E), (0, 0)))
    deltas = jnp.abs(jnp.linspace(math.log(HY_DECAY_TARGET) / HY_SLOW_DECAY,
                                  math.log(HY_DECAY_TARGET) / HY_FAST_DECAY, HY_W, dtype=F32))
    decay = jnp.where((n == L)[:, None], 0.0, jnp.exp(-t[:, None] * deltas[None, :]))
    k, norm = filter_mlp(feats, w1p, b1, f1, w2, b2, f2, w3, decay, bm=min(1024, L))
    W = HY_ORDER * HY_W
    a = matmul(m1f, k.reshape(FFT_N1, N2 * W), bm=2 * FFT_N1, bn=4096, out_dtype=BF16, name="hy_filter_dft1")
    return dft_stage2(gm, a.reshape(2, FFT_N1, N2, W), norm)


def _conv3_kernel(u_ref, prev_ref, next_ref, w_ref, b_ref, o_ref):
    i = pl.program_id(2)
    x = u_ref[...]
    bt = x.shape[0]
    row = lax.broadcasted_iota(jnp.int32, x.shape, 0)
    sub = prev_ref.shape[0]
    prev_row = jnp.where(i > 0, prev_ref[sub - 1:sub, :], 0.0)
    next_row = jnp.where(i < pl.num_programs(2) - 1, next_ref[0:1, :], 0.0)
    x_dn = jnp.where(row == 0, prev_row, pltpu.roll(x, 1, axis=0))
    x_up = jnp.where(row == bt - 1, next_row, pltpu.roll(x, bt - 1, axis=0))
    w = w_ref[...]
    o_ref[...] = w[0:1] * x_dn + w[1:2] * x + w[2:3] * x_up + b_ref[...]


def short_conv(u, conv_w, conv_b, *, bt):
    B, L, C = u.shape
    G = C // HY_W
    sub = 8
    bt = min(bt, L)
    r = bt // sub
    return pl.pallas_call(
        _conv3_kernel,
        out_shape=jax.ShapeDtypeStruct((G, B, L, HY_W), F32),
        grid=(B, G, L // bt),
        in_specs=[pl.BlockSpec((None, bt, HY_W), lambda b, g, i: (b, i, g)),
                  pl.BlockSpec((None, sub, HY_W), lambda b, g, i: (b, jnp.maximum(i * r - 1, 0), g)),
                  pl.BlockSpec((None, sub, HY_W), lambda b, g, i: (b, jnp.minimum((i + 1) * r, L // sub - 1), g)),
                  pl.BlockSpec((3, HY_W), lambda b, g, i: (0, g)),
                  pl.BlockSpec((1, HY_W), lambda b, g, i: (0, g))],
        out_specs=pl.BlockSpec((None, None, bt, HY_W), lambda b, g, i: (g, b, i, 0)),
        compiler_params=_cparams(("parallel", "parallel", "parallel")),
        name="hy_short_conv",
    )(u, u, u, conv_w, conv_b.reshape(1, C))


def hyena_mixer(u, conv_w, conv_b, w1, b1, f1, w2, b2, f2, w3, d_skip, tables):
    B, L, _ = u.shape
    assert B == 2
    m1, _, m4, gm, gmt = tables
    N2 = 2 * L // FFT_N1
    v, x1, x2 = short_conv(u, conv_w, conv_b, bt=1024)
    kf = hyena_filter_spectrum(L, w1, b1, f1, w2, b2, f2, w3, tables)
    rows = FFT_N1
    cols = N2 * HY_W
    z = v.reshape(rows, cols)
    for o, gate in enumerate((x1, x2)):
        a = matmul(m1, z, bm=2 * FFT_N1, bn=4096, out_dtype=BF16, name="hy_dft1")
        d = spectral_stage(gm, gmt, a.reshape(2, FFT_N1, N2, HY_W), kf, o)
        skip_row = jnp.tile(d_skip[o].astype(F32), 4096 // HY_W).reshape(1, 4096)
        z = inv_gate_stage(m4, d.reshape(2 * FFT_N1, cols), z, gate.reshape(rows, cols), skip_row, bn=4096)
    return z.reshape(B, L, HY_W)


def _na_bias_table(rpb):
    wc = NA_WIN_C
    cols = np.arange(GRID_W)
    cs = np.clip(cols - wc // 2, 0, GRID_W - wc)
    allowed = (cols[None, :] >= cs[:, None]) & (cols[None, :] < cs[:, None] + wc)
    dc = np.clip(cols[None, :] - cols[:, None] + (NA_WIN_C - 1), 0, 2 * NA_WIN_C - 2)
    off = np.arange(NA_WIN_R)
    di = np.arange(NA_WIN_R)[None, :] - off[:, None] + (NA_WIN_R - 1)
    tab = rpb.astype(F32)[:, di[:, :, None, None], dc[None, None, :, :]]
    tab = jnp.where(jnp.asarray(allowed)[None, None, None], tab, NEG_BIG)
    tab = jnp.transpose(tab, (1, 0, 3, 2, 4))
    return tab.reshape(NA_WIN_R, NA_HEADS, GRID_W, NA_WIN_R * GRID_W)


def _na_kernel(q_ref, k0_ref, k1_ref, k2_ref, v0_ref, v1_ref, v2_ref, bias_ref, o_ref, kbuf, vbuf, *, rows):
    j = pl.program_id(1)
    nb = rows // NA_BAND
    bt = NA_BAND * GRID_W
    base = jnp.clip(j - 1, 0, nb - 3)
    for n, (kr, vr) in enumerate(((k0_ref, v0_ref), (k1_ref, v1_ref), (k2_ref, v2_ref))):
        kbuf[n * bt:(n + 1) * bt, :] = kr[...]
        vbuf[n * bt:(n + 1) * bt, :] = vr[...]
    lane = lax.broadcasted_iota(jnp.int32, (1, 2 * NA_HEAD_DIM), 1)
    first = lane < NA_HEAD_DIM
    scale = NA_HEAD_DIM ** -0.5
    nkeys = NA_WIN_R * GRID_W

    def row_body(t, carry):
        r = j * NA_BAND + t
        rs = jnp.clip(r - NA_WIN_R // 2, 0, rows - NA_WIN_R)
        ko = pl.multiple_of((rs - NA_BAND * base) * GRID_W, GRID_W)
        offi = r - rs
        qo = pl.multiple_of(t * GRID_W, GRID_W)
        for hp in range(NA_HEADS // 2):
            ls = slice(hp * 2 * NA_HEAD_DIM, (hp + 1) * 2 * NA_HEAD_DIM)
            q2 = q_ref[pl.ds(qo, GRID_W), ls]
            k2 = kbuf[pl.ds(ko, nkeys), ls]
            v2 = vbuf[pl.ds(ko, nkeys), ls]
            outs = []
            for s in range(2):
                qm = jnp.where(first if s == 0 else jnp.logical_not(first), q2, jnp.zeros_like(q2))
                sc = lax.dot_general(qm, k2, _NT, preferred_element_type=F32) * scale
                sc = sc + bias_ref[offi, 2 * hp + s]
                m = jnp.max(sc, axis=-1, keepdims=True)
                p = jnp.exp(sc - m)
                l = jnp.sum(p, axis=-1, keepdims=True)
                o = jnp.dot(p.astype(BF16), v2, preferred_element_type=F32)
                outs.append(o / l)
            o_ref[pl.ds(qo, GRID_W), ls] = jnp.where(first, outs[0], outs[1])
        return carry

    lax.fori_loop(0, NA_BAND, row_body, 0)


def neighbourhood_attention(qkv, rpb):
    B, L, _ = qkv.shape
    rows = L // GRID_W
    assert rows % NA_BAND == 0 and rows >= 3 * NA_BAND
    nb = rows // NA_BAND
    bt = NA_BAND * GRID_W
    bias = _na_bias_table(rpb)

    def kv_spec(col, n):
        return pl.BlockSpec((None, bt, NA_W), lambda b, j: (b, jnp.clip(j - 1, 0, nb - 3) + n, col))

    return pl.pallas_call(
        functools.partial(_na_kernel, rows=rows),
        out_shape=jax.ShapeDtypeStruct((B, L, NA_W), F32),
        grid=(B, nb),
        in_specs=[pl.BlockSpec((None, bt, NA_W), lambda b, j: (b, j, 0)),
                  kv_spec(1, 0), kv_spec(1, 1), kv_spec(1, 2),
                  kv_spec(2, 0), kv_spec(2, 1), kv_spec(2, 2),
                  pl.BlockSpec(bias.shape, lambda b, j: (0, 0, 0, 0))],
        out_specs=pl.BlockSpec((None, bt, NA_W), lambda b, j: (b, j, 0)),
        scratch_shapes=[pltpu.VMEM((3 * bt, NA_W), BF16), pltpu.VMEM((3 * bt, NA_W), BF16)],
        compiler_params=_cparams(("parallel", "parallel")),
        name="na_attention",
    )(qkv, qkv, qkv, qkv, qkv, qkv, qkv, bias)


def _flash_kernel(q_ref, kn_ref, kr_ref, v_ref, o_ref, m_ref, l_ref, acc_ref, *, bk, unroll):
    nk = kn_ref.shape[0] // bk
    q = q_ref[...]
    m_ref[...] = jnp.full(m_ref.shape, -jnp.inf, F32)
    l_ref[...] = jnp.zeros(l_ref.shape, F32)
    acc_ref[...] = jnp.zeros(acc_ref.shape, F32)

    def body(kc, carry):
        off = pl.multiple_of(kc * bk, bk)
        k = jnp.concatenate([kn_ref[pl.ds(off, bk), :], kr_ref[pl.ds(off, bk), :]], axis=1)
        s = lax.dot_general(q, k, _NT, preferred_element_type=F32)
        m_prev = m_ref[...]
        m_new = jnp.maximum(m_prev, jnp.max(s, axis=-1, keepdims=True))
        alpha = jnp.exp2(m_prev - m_new)
        p = jnp.exp2((s - jnp.concatenate([m_new] * (bk // MLA_V), axis=1)).astype(BF16))
        l_ref[...] = alpha * l_ref[...] + jnp.sum(p.astype(F32), axis=-1, keepdims=True)
        acc_ref[...] = alpha * acc_ref[...] + jnp.dot(p, v_ref[pl.ds(off, bk), :], preferred_element_type=F32)
        m_ref[...] = m_new
        return carry

    lax.fori_loop(0, nk, body, 0, unroll=unroll)
    o_ref[...] = acc_ref[...] / l_ref[...]


def flash_attention(q, kv, krr, *, bq, bk):
    B, L, _ = q.shape
    H = MLA_HEADS
    bq, bk = min(bq, L), min(bk, L)
    dq = MLA_NOPE + 2 * MLA_ROPE
    seq = lambda col: pl.BlockSpec((None, L, MLA_V), col)
    return pl.pallas_call(
        functools.partial(_flash_kernel, bk=bk, unroll=min(4, L // bk)),
        out_shape=jax.ShapeDtypeStruct((B, L, H * MLA_V), F32),
        grid=(B, H, L // bq),
        in_specs=[pl.BlockSpec((None, bq, dq), lambda b, h, i: (b, i, h)),
                  seq(lambda b, h, i: (b, 0, h)),
                  seq(lambda b, h, i: (b, 0, 0)),
                  seq(lambda b, h, i: (b, 0, H + h))],
        out_specs=pl.BlockSpec((None, bq, MLA_V), lambda b, h, i: (b, i, h)),
        scratch_shapes=[pltpu.VMEM((bq, MLA_V), F32), pltpu.VMEM((bq, MLA_V), F32), pltpu.VMEM((bq, MLA_V), F32)],
        compiler_params=_cparams(("parallel", "parallel", "arbitrary")),
        name="mla_flash",
    )(q, kv, krr, kv)


def latent_attention(c, q_norm_g, w_uq, kv_norm_g, w_ukv, B, L):
    H = MLA_HEADS
    k_rope = c[:, MLA_Q_RANK + MLA_KV_RANK:].reshape(B, L, MLA_ROPE)
    pos = jnp.arange(L, dtype=F32)
    inv_freq = ROPE_BASE ** (-jnp.arange(0, MLA_ROPE, 2, dtype=F32) / MLA_ROPE)
    ang = pos[:, None] * inv_freq[None, :]
    cos, sin = jnp.cos(ang), jnp.sin(ang)
    cos2 = jnp.concatenate([cos, cos], -1)
    sin2 = jnp.concatenate([sin, sin], -1)
    half = MLA_ROPE // 2
    wq = w_uq.reshape(MLA_Q_RANK, H, MLA_NOPE + MLA_ROPE)
    w_rot = wq[:, :, MLA_NOPE:]
    w_swap = jnp.concatenate([-w_rot[:, :, half:], w_rot[:, :, :half]], -1)
    wq2 = jnp.concatenate([wq, w_swap], -1).reshape(MLA_Q_RANK, H * (MLA_NOPE + 2 * MLA_ROPE)).astype(BF16)
    scale = (MLA_NOPE + MLA_ROPE) ** -0.5 * math.log2(math.e)
    qtab = jnp.concatenate([jnp.ones((L, MLA_NOPE), F32), cos2, sin2], -1) * scale
    q = rms_matmul(c, 0, MLA_Q_RANK, q_norm_g, wq2, qtab, bm=min(512, L), seq_len=L, out_dtype=BF16,
                   name="mla_q_up")
    wkv = w_ukv.reshape(MLA_KV_RANK, H, MLA_NOPE + MLA_V)
    wkv2 = jnp.concatenate([wkv[:, :, :MLA_NOPE].reshape(MLA_KV_RANK, H * MLA_NOPE),
                            wkv[:, :, MLA_NOPE:].reshape(MLA_KV_RANK, H * MLA_V)], -1).astype(BF16)
    kv = rms_matmul(c, MLA_Q_RANK // MLA_KV_RANK, MLA_KV_RANK, kv_norm_g, wkv2, None, bm=min(512, L), seq_len=L,
                    out_dtype=BF16, name="mla_kv_up")
    x1, x2 = k_rope[..., :half], k_rope[..., half:]
    kr = jnp.concatenate([x1 * cos - x2 * sin, x1 * sin + x2 * cos], -1)
    krr = jnp.concatenate([kr, kr], -1).astype(BF16)
    return flash_attention(q.reshape(B, L, -1), kv.reshape(B, L, -1), krr, bq=1024, bk=512)


def _layer_norm(r, g, b):
    mu = jnp.mean(r, axis=-1, keepdims=True)
    d = r - mu
    var = jnp.mean(d * d, axis=-1, keepdims=True)
    return d * lax.rsqrt(var + EPS) * g + b


def _out_proj_kernel(hy_ref, na_ref, ml_ref, gg_ref, w_ref, x_ref, g_ref, b_ref, o_ref):
    def rms(y, g):
        return (y * lax.rsqrt(jnp.mean(y * y, axis=-1, keepdims=True) + EPS) * g).astype(BF16)

    gg = gg_ref[...]
    mix = jnp.concatenate([rms(hy_ref[...], gg[:, :HY_W]),
                           rms(na_ref[...], gg[:, HY_W:HY_W + NA_W]),
                           rms(ml_ref[...], gg[:, HY_W + NA_W:])], axis=1)
    acc = jnp.dot(mix, w_ref[...], preferred_element_type=F32)
    o_ref[...] = _layer_norm(ALPHA * x_ref[...] + acc, g_ref[...], b_ref[...])


def out_projection(y_hy, y_na, y_ml, grp_g, w_out_bf16, x, ln_g, ln_b, *, bm):
    T, D = x.shape
    row = lambda w: pl.BlockSpec((bm, w), lambda i: (i, 0))
    full = lambda shp: pl.BlockSpec(shp, lambda i: (0, 0))
    return pl.pallas_call(
        _out_proj_kernel,
        out_shape=jax.ShapeDtypeStruct((T, D), F32),
        grid=(T // bm,),
        in_specs=[row(HY_W), row(NA_W), row(MLA_W), full((1, D)), full(w_out_bf16.shape), row(D),
                  full((1, D)), full((1, D))],
        out_specs=row(D),
        compiler_params=_cparams(("parallel",)),
        name="out_proj_ln",
    )(y_hy, y_na, y_ml, grp_g.reshape(1, D), w_out_bf16, x, ln_g.reshape(1, D), ln_b.reshape(1, D))


def _router_kernel(x_ref, wh_ref, wl_ref, b_ref, tri_ref, idx_ref, gate_ref, rank_ref, cnt_ref, run_ref):
    x = x_ref[...]
    xh = x.astype(BF16)
    xl = (x - xh.astype(F32)).astype(BF16)
    wh, wl = wh_ref[...], wl_ref[...]
    lg = (lax.dot_general(wh, xh, _NT, preferred_element_type=F32)
          + lax.dot_general(wh, xl, _NT, preferred_element_type=F32)
          + lax.dot_general(wl, xh, _NT, preferred_element_type=F32))
    bm = lg.shape[1]
    per = N_EXPERTS // N_GROUPS
    sc = 1.0 / (1.0 + jnp.exp(-lg))
    sel = (sc + b_ref[...]).reshape(N_GROUPS, per, bm)
    sub = lax.broadcasted_iota(jnp.int32, sel.shape, 1).astype(F32)
    m1 = jnp.max(sel, axis=1, keepdims=True)
    i1 = jnp.min(jnp.where(sel == m1, sub, float(per)), axis=1, keepdims=True)
    rest = jnp.where(sub == i1, -jnp.inf, sel)
    m2 = jnp.max(rest, axis=1, keepdims=True)
    i2 = jnp.min(jnp.where(rest == m2, sub, float(per)), axis=1, keepdims=True)
    grp = (m1 + m2)[:, 0, :]
    gi = lax.broadcasted_iota(jnp.int32, grp.shape, 0).astype(F32)
    gmax = jnp.max(grp, axis=0, keepdims=True)
    best = jnp.min(jnp.where(grp == gmax, gi, float(N_GROUPS)), axis=0, keepdims=True)
    onb = gi == best
    e1 = jnp.sum(jnp.where(onb, gi * per + i1[:, 0, :], 0.0), axis=0, keepdims=True)
    e2 = jnp.sum(jnp.where(onb, gi * per + i2[:, 0, :], 0.0), axis=0, keepdims=True)
    ei = lax.broadcasted_iota(jnp.int32, sc.shape, 0).astype(F32)
    g1 = jnp.sum(jnp.where(ei == e1, sc, 0.0), axis=0, keepdims=True)
    g2 = jnp.sum(jnp.where(ei == e2, sc, 0.0), axis=0, keepdims=True)
    den = g1 + g2
    idx_ref[...] = jnp.concatenate([e1, e2], axis=0).astype(jnp.int32)
    gate_ref[...] = jnp.concatenate([g1 / den, g2 / den], axis=0)

    @pl.when(pl.program_id(0) == 0)
    def _():
        run_ref[...] = jnp.zeros(run_ref.shape, F32)

    hit1, hit2 = ei == e1, ei == e2
    onehot = jnp.where(jnp.logical_or(hit1, hit2), 1.0, 0.0)
    before = jnp.dot(onehot.astype(BF16), tri_ref[...], preferred_element_type=F32) + run_ref[:, 0:1]
    r1 = jnp.sum(jnp.where(hit1, before, 0.0), axis=0, keepdims=True)
    r2 = jnp.sum(jnp.where(hit2, before, 0.0), axis=0, keepdims=True)
    rank_ref[...] = jnp.concatenate([r1, r2], axis=0).astype(jnp.int32)
    run_ref[...] = run_ref[...] + jnp.sum(onehot, axis=1, keepdims=True)
    cnt_ref[...] = run_ref[...].astype(jnp.int32)


def router(x, router_w, router_b, *, bm):
    T, D = x.shape
    wt = router_w.astype(F32).T
    wh = wt.astype(BF16)
    wl = (wt - wh.astype(F32)).astype(BF16)
    tri = (np.arange(bm)[:, None] < np.arange(bm)[None, :]).astype(np.float32)
    tok = lambda dt: jax.ShapeDtypeStruct((TOP_K, T), dt)
    tok_spec = pl.BlockSpec((TOP_K, bm), lambda i: (0, i))
    return pl.pallas_call(
        _router_kernel,
        out_shape=(tok(jnp.int32), tok(F32), tok(jnp.int32), jax.ShapeDtypeStruct((N_EXPERTS, 128), jnp.int32)),
        grid=(T // bm,),
        in_specs=[pl.BlockSpec((bm, D), lambda i: (i, 0)),
                  pl.BlockSpec((N_EXPERTS, D), lambda i: (0, 0)),
                  pl.BlockSpec((N_EXPERTS, D), lambda i: (0, 0)),
                  pl.BlockSpec((N_EXPERTS, 1), lambda i: (0, 0)),
                  pl.BlockSpec((bm, bm), lambda i: (0, 0))],
        out_specs=(tok_spec, tok_spec, tok_spec, pl.BlockSpec((N_EXPERTS, 128), lambda i: (0, 0))),
        scratch_shapes=[pltpu.VMEM((N_EXPERTS, 128), F32)],
        compiler_params=_cparams(("arbitrary",)),
        name="moe_router",
    )(x, wh, wl, router_b.astype(F32).reshape(N_EXPERTS, 1), jnp.asarray(tri, BF16))


def _expert_kernel(blk_e_ref, nused_ref, cnt_ref, a_cur_ref, a_nxt_ref, x_hbm, wg_ref, wu_ref, wd_ref, o_hbm,
                   wg_s, wu_s, wd_s, xbuf, ybuf, sem_in, sem_out):
    i = pl.program_id(0)
    slot = i % 2
    n_used = nused_ref[0]
    bs = MOE_BLOCK

    def rows(buf, sl, r, n):
        return buf.at[pl.ds(pl.multiple_of((sl * bs + r) * ROW_SUB, ROW_SUB), n * ROW_SUB)]

    def gather_count(blk):
        return (cnt_ref[blk] + 7) // 8 * 8

    def gather(a_ref, blk, sl):
        def issue8(j, c):
            for q in range(8):
                r = j * 8 + q
                tok = a_ref[0, r] // TOP_K
                pltpu.make_async_copy(x_hbm.at[tok], rows(xbuf, sl, r, 1), sem_in.at[sl]).start()
            return c
        lax.fori_loop(0, gather_count(blk) // 8, issue8, 0)

    def wait_rows(buf, sem, n, sl):
        r = rows(buf, sl, 0, n)
        pltpu.make_async_copy(r, r, sem.at[sl]).wait()

    wait_gather = lambda blk, sl: wait_rows(xbuf, sem_in, gather_count(blk), sl)
    wait_scatter = lambda blk, sl: wait_rows(ybuf, sem_out, cnt_ref[blk], sl)

    @pl.when(i == 0)
    def _():
        xbuf[...] = jnp.zeros(xbuf.shape, F32)
        gather(a_cur_ref, 0, 0)

    @pl.when(i + 1 < n_used)
    def _():
        gather(a_nxt_ref, i + 1, 1 - slot)

    prev = blk_e_ref[jnp.maximum(i - 1, 0)]

    @pl.when(jnp.logical_or(i == 0, blk_e_ref[i] != prev))
    def _():
        wg_s[...] = wg_ref[...].astype(BF16)
        wu_s[...] = wu_ref[...].astype(BF16)
        wd_s[...] = wd_ref[...].astype(BF16)

    @pl.when(i < n_used)
    def _():
        wait_gather(i, slot)
        base = pl.multiple_of(slot * bs * ROW_SUB, ROW_SUB)
        g = jnp.zeros((bs, D_EXPERT), F32)
        u = jnp.zeros((bs, D_EXPERT), F32)
        for s in range(ROW_SUB):
            xs = xbuf[pl.ds(base + s, bs, stride=ROW_SUB), :].astype(BF16)
            g = g + jnp.dot(xs, wg_s[s * 128:(s + 1) * 128, :], preferred_element_type=F32)
            u = u + jnp.dot(xs, wu_s[s * 128:(s + 1) * 128, :], preferred_element_type=F32)
        h = (g * (1.0 / (1.0 + jnp.exp(-g))) * u).astype(BF16)
        y = jnp.dot(h, wd_s[...], preferred_element_type=F32)

        @pl.when(i >= 2)
        def _():
            wait_scatter(i - 2, slot)

        for s in range(ROW_SUB):
            ybuf[pl.ds(base + s, bs, stride=ROW_SUB), :] = y[:, s * 128:(s + 1) * 128]

        def issue(r, c):
            pltpu.make_async_copy(rows(ybuf, slot, r, 1), o_hbm.at[a_cur_ref[0, r]], sem_out.at[slot]).start()
            return c
        lax.fori_loop(0, cnt_ref[i], issue, 0)

        @pl.when(i == n_used - 1)
        def _():
            wait_scatter(i, slot)

            @pl.when(i >= 1)
            def _():
                wait_scatter(i - 1, 1 - slot)


def expert_ffn(x3, buf_a, blk_e, n_used, cnt_blk, w_gate, w_up, w_down):
    T = x3.shape[0]
    D = ROW_SUB * 128
    nblk = buf_a.shape[0]
    last = nblk - 1
    smem_blk = lambda f: pl.BlockSpec((None, 1, MOE_BLOCK), f, memory_space=pltpu.SMEM)
    wspec = lambda shp: pl.BlockSpec((None,) + shp, lambda i, be, nu, cn: (be[i], 0, 0))
    grid_spec = pltpu.PrefetchScalarGridSpec(
        num_scalar_prefetch=3,
        grid=(nblk,),
        in_specs=[smem_blk(lambda i, be, nu, cn: (i, 0, 0)),
                  smem_blk(lambda i, be, nu, cn: (jnp.minimum(i + 1, last), 0, 0)),
                  pl.BlockSpec(memory_space=pl.ANY),
                  wspec((D, D_EXPERT)), wspec((D, D_EXPERT)), wspec((D_EXPERT, D))],
        out_specs=pl.BlockSpec(memory_space=pl.ANY),
        scratch_shapes=[pltpu.VMEM((D, D_EXPERT), BF16), pltpu.VMEM((D, D_EXPERT), BF16),
                        pltpu.VMEM((D_EXPERT, D), BF16),
                        pltpu.VMEM((2 * MOE_BLOCK * ROW_SUB, 128), F32), pltpu.VMEM((2 * MOE_BLOCK * ROW_SUB, 128), F32),
                        pltpu.SemaphoreType.DMA((2,)), pltpu.SemaphoreType.DMA((2,))],
    )
    return pl.pallas_call(
        _expert_kernel,
        out_shape=jax.ShapeDtypeStruct((T * TOP_K, ROW_SUB, 128), F32),
        grid_spec=grid_spec,
        compiler_params=_cparams(("arbitrary",)),
        name="moe_experts",
    )(blk_e, n_used, cnt_blk, buf_a, buf_a, x3, w_gate, w_up, w_down)


def _combine_ln_kernel(x_ref, y_ref, gt_ref, g_ref, b_ref, o_ref):
    bm = x_ref.shape[0]
    gt = gt_ref[...]
    g0, g1 = gt[:, 0:1], gt[:, 1:2]
    per_tok = TOP_K * ROW_SUB
    ff = jnp.concatenate(
        [g0 * y_ref[pl.ds(s, bm, stride=per_tok), :] + g1 * y_ref[pl.ds(ROW_SUB + s, bm, stride=per_tok), :]
         for s in range(ROW_SUB)], axis=1)
    o_ref[...] = _layer_norm(ALPHA * x_ref[...] + ff, g_ref[...], b_ref[...])


def combine_ln(x, y3, gate_t, ln_g, ln_b, *, bm):
    T, D = x.shape
    row = lambda w: pl.BlockSpec((bm, w), lambda i: (i, 0))
    full = lambda shp: pl.BlockSpec(shp, lambda i: (0, 0))
    return pl.pallas_call(
        _combine_ln_kernel,
        out_shape=jax.ShapeDtypeStruct((T, D), F32),
        grid=(T // bm,),
        in_specs=[row(D), pl.BlockSpec((bm * TOP_K * ROW_SUB, 128), lambda i: (i, 0)), row(TOP_K),
                  full((1, D)), full((1, D))],
        out_specs=row(D),
        compiler_params=_cparams(("parallel",)),
        name="moe_combine_ln",
    )(x, y3.reshape(T * TOP_K * ROW_SUB, 128), gate_t, ln_g.reshape(1, D), ln_b.reshape(1, D))


def grouped_moe_ln(x, router_w, router_b, w_gate, w_up, w_down, ln_g, ln_b):
    T, D = x.shape
    assert D == ROW_SUB * 128
    idx, gate, rank, counts = router(x, router_w, router_b, bm=min(512, T))
    A = T * TOP_K
    counts = counts[:, 0]
    padded = (counts + MOE_BLOCK - 1) // MOE_BLOCK * MOE_BLOCK
    pends = jnp.cumsum(padded)
    pstarts = pends - padded
    P = A + N_EXPERTS * MOE_BLOCK
    nblk = P // MOE_BLOCK
    blk_start = jnp.arange(nblk, dtype=jnp.int32) * MOE_BLOCK
    blk_e = jnp.minimum(jnp.searchsorted(pends, blk_start, side='right'), N_EXPERTS - 1).astype(jnp.int32)
    n_used = (pends[-1] // MOE_BLOCK).astype(jnp.int32).reshape(1)
    cnt_blk = jnp.clip(counts[blk_e] - (blk_start - pstarts[blk_e]), 0, MOE_BLOCK).astype(jnp.int32)
    dest = (pstarts[idx] + rank).astype(jnp.int32)
    a_id = jnp.arange(T, dtype=jnp.int32)[None, :] * TOP_K + jnp.arange(TOP_K, dtype=jnp.int32)[:, None]
    buf_a = jnp.zeros((P,), jnp.int32).at[dest.reshape(A)].set(a_id.reshape(A))
    y3 = expert_ffn(x.reshape(T, ROW_SUB, 128), buf_a.reshape(nblk, 1, MOE_BLOCK), blk_e, n_used, cnt_blk,
                    w_gate, w_up, w_down)
    return combine_ln(x, y3, gate.T, ln_g, ln_b, bm=min(256, T))


def kernel(x, w_in, conv_w, conv_b, hy_w1, hy_b1, hy_f1, hy_w2, hy_b2, hy_f2, hy_w3, hy_d,
           na_rpb, q_norm_g, w_uq, kv_norm_g, w_ukv, grp_norm_g, w_out, ln1_g, ln1_b,
           router_w, router_b, exp_w_gate, exp_w_up, exp_w_down, ln2_g, ln2_b):
    B, L, D = x.shape
    T = B * L
    tables = _dft_tables(L)
    s1 = 3 * HY_W
    s2 = s1 + 3 * NA_W
    xt = x.reshape(T, D)
    for l in range(DEPTH):
        w = w_in[l].astype(BF16)
        p_hy = matmul(xt, w[:, :s1], bm=1024, bn=512, out_dtype=F32, name="in_proj_hy")
        p_na = matmul(xt, w[:, s1:s2], bm=1024, bn=512, out_dtype=BF16, name="in_proj_na")
        p_c = matmul(xt, w[:, s2:], bm=1024, bn=w.shape[1] - s2, out_dtype=F32, name="in_proj_mla")
        y_hy = hyena_mixer(p_hy.reshape(B, L, s1), conv_w[l], conv_b[l], hy_w1[l], hy_b1[l], hy_f1[l],
                           hy_w2[l], hy_b2[l], hy_f2[l], hy_w3[l], hy_d[l], tables)
        y_na = neighbourhood_attention(p_na.reshape(B, L, 3 * NA_W), na_rpb[l])
        y_ml = latent_attention(p_c, q_norm_g[l], w_uq[l], kv_norm_g[l], w_ukv[l], B, L)
        xt = out_projection(y_hy.reshape(T, HY_W), y_na.reshape(T, NA_W), y_ml.reshape(T, MLA_W),
                            grp_norm_g[l], w_out[l].astype(BF16), xt, ln1_g[l], ln1_b[l], bm=256)
        xt = grouped_moe_ln(xt, router_w, router_b, exp_w_gate[l], exp_w_up[l], exp_w_down[l],
                            ln2_g[l], ln2_b[l])
    return xt.reshape(B, L, D)
```
